```python
import math
import jax, jax.numpy as jnp
from jax import lax
import numpy as np

D_MODEL = 2048
BATCH = 8
SEQ = 4096
DEPTH = 2
DEC_BATCH = 8
DEC_SEQ = 64
PAST_LEN = 1024

CHUNK = 64
N_MIXERS = 2
N_A_LAYERS = (DEPTH + 1) // 2
N_B_LAYERS = DEPTH // 2
EPS = 1e-6
NEG_INF = -1e30

A_HEADS = 16
A_HEAD_DIM = D_MODEL // A_HEADS
A_WIDTH = A_HEADS * A_HEAD_DIM
A_LEFT_CHUNKS = 8
A_WINDOW = A_LEFT_CHUNKS * CHUNK
A_BAND = A_WINDOW + CHUNK
REL_CLIP = 128
A_SCALE = A_HEAD_DIM ** -0.5

B_HEADS = 16
B_NOPE = 128
B_ROPE = 64
B_V = 128
B_QK = B_NOPE + B_ROPE
B_Q_RANK = D_MODEL // 4
B_KV_RANK = 512
B_WIDTH = B_HEADS * B_V
B_IN = B_Q_RANK + B_KV_RANK + B_ROPE + B_WIDTH
B_SCALE = B_QK ** -0.5
ROPE_THETA = 10000.0
Q_BLOCK = 128

kernel_name = 'hybrid_chunkband_mla_streaming_step'


def rmsnorm(x, g):
    xf = x.astype(jnp.float32)
    y = xf * lax.rsqrt(jnp.mean(xf * xf, axis=-1, keepdims=True) + EPS)
    return (y * g.astype(jnp.float32)).astype(x.dtype)


def rel_bias(table, q_pos, k_pos):
    idx = jnp.clip(q_pos[:, None] - k_pos[None, :], -REL_CLIP, REL_CLIP) + REL_CLIP
    return jnp.take(table, idx, axis=1).astype(jnp.float32)


def rope(x, pos):
    half = x.shape[-1] // 2
    inv = ROPE_THETA ** (-jnp.arange(half, dtype=jnp.float32) / half)
    ang = pos.astype(jnp.float32)[:, None] * inv[None, :]
    shape = (pos.shape[0],) + (1,) * (x.ndim - 3) + (half,)
    cos = jnp.cos(ang).reshape(shape)
    sin = jnp.sin(ang).reshape(shape)
    xf = x.astype(jnp.float32)
    x1, x2 = xf[..., :half], xf[..., half:]
    return jnp.concatenate([x1 * cos - x2 * sin, x1 * sin + x2 * cos], axis=-1).astype(x.dtype)


def a_project(h, w_in, g_q, g_k):
    b, s, _ = h.shape
    q, k, v, g = jnp.split(h @ w_in, 4, axis=-1)
    hd = (b, s, A_HEADS, A_HEAD_DIM)
    return rmsnorm(q.reshape(hd), g_q), rmsnorm(k.reshape(hd), g_k), v.reshape(hd), g


def a_prompt(x, ln, w_in, g_q, g_k, table, w_out):
    b, s, _ = x.shape
    nc = s // CHUNK
    q, k, v, g = a_project(rmsnorm(x, ln), w_in, g_q, g_k)
    pad = ((0, 0), (A_WINDOW, 0), (0, 0), (0, 0))
    kp = jnp.pad(k, pad)
    vp = jnp.pad(v, pad)
    qc = jnp.moveaxis(q.reshape(b, nc, CHUNK, A_HEADS, A_HEAD_DIM), 1, 0)
    band = jnp.arange(A_BAND)
    bias = rel_bias(table, jnp.arange(CHUNK) + A_WINDOW, band)

    def one_chunk(args):
        c, qb = args
        start = c * CHUNK
        kb = lax.dynamic_slice_in_dim(kp, start, A_BAND, axis=1)
        vb = lax.dynamic_slice_in_dim(vp, start, A_BAND, axis=1)
        sc = jnp.einsum('bqhd,bkhd->bhqk', qb, kb, preferred_element_type=jnp.float32) * A_SCALE + bias
        sc = jnp.where(start + band >= A_WINDOW, sc, NEG_INF)
        p = jax.nn.softmax(sc, axis=-1).astype(vb.dtype)
        return jnp.einsum('bhqk,bkhd->bqhd', p, vb)

    o = lax.map(one_chunk, (jnp.arange(nc), qc))
    o = jnp.moveaxis(o, 0, 1).reshape(b, s, A_WIDTH)
    y = x + (jax.nn.silu(g) * o) @ w_out
    keep = min(A_WINDOW, s)
    return y, k[:, s - keep:], v[:, s - keep:]


def a_sample(x, k_cache, v_cache, ln, w_in, g_q, g_k, table, w_out):
    b, s, _ = x.shape
    n_cache = k_cache.shape[1]
    q, k, v, g = a_project(rmsnorm(x, ln), w_in, g_q, g_k)
    kk = jnp.concatenate([k_cache.astype(k.dtype), k], axis=1)
    vv = jnp.concatenate([v_cache.astype(v.dtype), v], axis=1)
    q_pos = PAST_LEN + jnp.arange(s)
    k_pos = jnp.concatenate([PAST_LEN - n_cache + jnp.arange(n_cache), q_pos])
    bias = rel_bias(table, q_pos, k_pos)
    sc = jnp.einsum('bqhd,bkhd->bhqk', q, kk, preferred_element_type=jnp.float32) * A_SCALE + bias
    p = jax.nn.softmax(sc, axis=-1).astype(vv.dtype)
    o = jnp.einsum('bhqk,bkhd->bqhd', p, vv).reshape(b, s, A_WIDTH)
    y = x + (jax.nn.silu(g) * o) @ w_out
    return y, k, v


def b_project(h, pos, w_in, g_qa, w_uq, g_kva, g_qn, g_qr, g_kr):
    b, s, _ = h.shape
    cq, ckv, kr, g = jnp.split(h @ w_in, [B_Q_RANK, B_Q_RANK + B_KV_RANK, B_Q_RANK + B_KV_RANK + B_ROPE], axis=-1)
    q = (rmsnorm(cq, g_qa) @ w_uq).reshape(b, s, B_HEADS, B_QK)
    q_nope = rmsnorm(q[..., :B_NOPE], g_qn)
    q_rope = rope(rmsnorm(q[..., B_NOPE:], g_qr), pos)
    ckv = rmsnorm(ckv, g_kva)
    k_rope = rope(rmsnorm(kr, g_kr), pos)
    return q_nope, q_rope, ckv, k_rope, g


def b_expand(ckv, w_uk, w_uv, g_kn):
    b, s, _ = ckv.shape
    k_nope = rmsnorm((ckv @ w_uk).reshape(b, s, B_HEADS, B_NOPE), g_kn)
    v = (ckv @ w_uv).reshape(b, s, B_HEADS, B_V)
    return k_nope, v


def b_scores(q_nope, q_rope, k_nope, k_rope):
    sn = jnp.einsum('bqhd,bkhd->bhqk', q_nope, k_nope, preferred_element_type=jnp.float32)
    sr = jnp.einsum('bqhr,bkr->bhqk', q_rope, k_rope, preferred_element_type=jnp.float32)
    return (sn + sr) * B_SCALE


def b_prompt(x, ln, w_in, g_qa, w_uq, g_kva, w_uk, w_uv, g_qn, g_kn, g_qr, g_kr, w_out):
    b, s, _ = x.shape
    pos = jnp.arange(s)
    q_nope, q_rope, ckv, k_rope, g = b_project(rmsnorm(x, ln), pos, w_in, g_qa, w_uq, g_kva, g_qn, g_qr, g_kr)
    k_nope, v = b_expand(ckv, w_uk, w_uv, g_kn)
    nb = s // Q_BLOCK
    qn = jnp.moveaxis(q_nope.reshape(b, nb, Q_BLOCK, B_HEADS, B_NOPE), 1, 0)
    qr = jnp.moveaxis(q_rope.reshape(b, nb, Q_BLOCK, B_HEADS, B_ROPE), 1, 0)
    k_chunk = pos // CHUNK

    def one_block(args):
        blk, qnb, qrb = args
        q_chunk = (blk * Q_BLOCK + jnp.arange(Q_BLOCK)) // CHUNK
        sc = b_scores(qnb, qrb, k_nope, k_rope)
        sc = jnp.where(k_chunk[None, :] <= q_chunk[:, None], sc, NEG_INF)
        p = jax.nn.softmax(sc, axis=-1).astype(v.dtype)
        return jnp.einsum('bhqk,bkhd->bqhd', p, v)

    o = lax.map(one_block, (jnp.arange(nb), qn, qr))
    o = jnp.moveaxis(o, 0, 1).reshape(b, s, B_WIDTH)
    y = x + (jax.nn.silu(g) * o) @ w_out
    return y, ckv, k_rope


def b_sample(x, ckv_cache, kr_cache, ln, w_in, g_qa, w_uq, g_kva, w_uk, w_uv, g_qn, g_kn, g_qr, g_kr, w_out):
    b, s, _ = x.shape
    pos = PAST_LEN + jnp.arange(s)
    q_nope, q_rope, ckv, k_rope, g = b_project(rmsnorm(x, ln), pos, w_in, g_qa, w_uq, g_kva, g_qn, g_qr, g_kr)
    ckv_all = jnp.concatenate([ckv_cache.astype(ckv.dtype), ckv], axis=1)
    kr_all = jnp.concatenate([kr_cache.astype(k_rope.dtype), k_rope], axis=1)
    k_nope, v = b_expand(ckv_all, w_uk, w_uv, g_kn)
    sc = b_scores(q_nope, q_rope, k_nope, kr_all)
    p = jax.nn.softmax(sc, axis=-1).astype(v.dtype)
    o = jnp.einsum('bhqk,bkhd->bqhd', p, v).reshape(b, s, B_WIDTH)
    y = x + (jax.nn.silu(g) * o) @ w_out
    return y, ckv, k_rope


def setup_inputs(seed: int = 0) -> dict:
    key = jax.random.key(seed)
    ks = jax.random.split(key, 26)
    f32 = jnp.float32

    def nrm(k, shape, scale):
        return jax.random.normal(k, shape, f32) * scale

    def gain(k, shape):
        return 1.0 + 0.01 * jax.random.normal(k, shape, f32)

    a_cache = min(A_WINDOW, PAST_LEN)
    na, nb = N_A_LAYERS, N_B_LAYERS
    return {
        'x_prompt': nrm(ks[0], (BATCH, SEQ, D_MODEL), 1.0),
        'x_sample': nrm(ks[1], (DEC_BATCH, DEC_SEQ, D_MODEL), 1.0),
        'cache_a_k': nrm(ks[2], (na, DEC_BATCH, a_cache, A_HEADS, A_HEAD_DIM), 1.0),
        'cache_a_v': nrm(ks[3], (na, DEC_BATCH, a_cache, A_HEADS, A_HEAD_DIM), 1.0),
        'cache_b_ckv': nrm(ks[4], (nb, DEC_BATCH, PAST_LEN, B_KV_RANK), 1.0),
        'cache_b_krope': nrm(ks[5], (nb, DEC_BATCH, PAST_LEN, B_ROPE), 1.0),
        'a_ln': gain(ks[6], (na, D_MODEL)),
        'w_a_in': nrm(ks[7], (na, D_MODEL, 4 * A_WIDTH), D_MODEL ** -0.5),
        'a_q_norm': gain(ks[8], (na, A_HEAD_DIM)),
        'a_k_norm': gain(ks[9], (na, A_HEAD_DIM)),
        'a_rel_bias': nrm(ks[10], (na, A_HEADS, 2 * REL_CLIP + 1), 0.2),
        'w_a_out': nrm(ks[11], (na, A_WIDTH, D_MODEL), A_WIDTH ** -0.5),
        'b_ln': gain(ks[12], (nb, D_MODEL)),
        'w_b_in': nrm(ks[13], (nb, D_MODEL, B_IN), D_MODEL ** -0.5),
        'b_q_a_norm': gain(ks[14], (nb, B_Q_RANK)),
        'w_b_uq': nrm(ks[15], (nb, B_Q_RANK, B_HEADS * B_QK), B_Q_RANK ** -0.5),
        'b_kv_a_norm': gain(ks[16], (nb, B_KV_RANK)),
        'w_b_uk': nrm(ks[17], (nb, B_KV_RANK, B_HEADS * B_NOPE), B_KV_RANK ** -0.5),
        'w_b_uv': nrm(ks[18], (nb, B_KV_RANK, B_HEADS * B_V), B_KV_RANK ** -0.5),
        'b_q_nope_norm': gain(ks[19], (nb, B_NOPE)),
        'b_k_nope_norm': gain(ks[20], (nb, B_NOPE)),
        'b_q_rope_norm': gain(ks[21], (nb, B_ROPE)),
        'b_k_rope_norm': gain(ks[22], (nb, B_ROPE)),
        'w_b_out': nrm(ks[23], (nb, B_WIDTH, D_MODEL), B_WIDTH ** -0.5),
    }


def reference(x_prompt, x_sample, cache_a_k, cache_a_v, cache_b_ckv, cache_b_krope,
              a_ln, w_a_in, a_q_norm, a_k_norm, a_rel_bias, w_a_out,
              b_ln, w_b_in, b_q_a_norm, w_b_uq, b_kv_a_norm, w_b_uk, w_b_uv,
              b_q_nope_norm, b_k_nope_norm, b_q_rope_norm, b_k_rope_norm, w_b_out):
    yp, ys = x_prompt, x_sample
    akp, avp, aks, avs = [], [], [], []
    bcp, brp, bcs, brs = [], [], [], []
    for layer in range(DEPTH):
        i = layer // N_MIXERS
        if layer % N_MIXERS == 0:
            yp, k_p, v_p = a_prompt(yp, a_ln[i], w_a_in[i], a_q_norm[i], a_k_norm[i], a_rel_bias[i], w_a_out[i])
            ys, k_s, v_s = a_sample(ys, cache_a_k[i], cache_a_v[i], a_ln[i], w_a_in[i], a_q_norm[i],
                                    a_k_norm[i], a_rel_bias[i], w_a_out[i])
            akp.append(k_p); avp.append(v_p); aks.append(k_s); avs.append(v_s)
        else:
            yp, c_p, r_p = b_prompt(yp, b_ln[i], w_b_in[i], b_q_a_norm[i], w_b_uq[i], b_kv_a_norm[i],
                                    w_b_uk[i], w_b_uv[i], b_q_nope_norm[i], b_k_nope_norm[i],
                                    b_q_rope_norm[i], b_k_rope_norm[i], w_b_out[i])
            ys, c_s, r_s = b_sample(ys, cache_b_ckv[i], cache_b_krope[i], b_ln[i], w_b_in[i], b_q_a_norm[i],
                                    w_b_uq[i], b_kv_a_norm[i], w_b_uk[i], w_b_uv[i], b_q_nope_norm[i],
                                    b_k_nope_norm[i], b_q_rope_norm[i], b_k_rope_norm[i], w_b_out[i])
            bcp.append(c_p); brp.append(r_p); bcs.append(c_s); brs.append(r_s)
    return (yp, ys,
            jnp.stack(akp), jnp.stack(avp), jnp.stack(bcp), jnp.stack(brp),
            jnp.stack(aks), jnp.stack(avs), jnp.stack(bcs), jnp.stack(brs))
```

```python
import functools

import jax
import jax.numpy as jnp
from jax import lax
from jax.experimental import pallas as pl
from jax.experimental.pallas import tpu as pltpu

D_MODEL = 2048
CHUNK = 64
EPS = 1e-6
NEG_INF = -1e30
PAST_LEN = 1024

A_HEADS = 16
A_HEAD_DIM = 128
A_WIDTH = A_HEADS * A_HEAD_DIM
A_WINDOW = 8 * CHUNK
A_BAND = A_WINDOW + CHUNK
REL_CLIP = 128
A_SCALE = A_HEAD_DIM ** -0.5

B_HEADS = 16
B_NOPE = 128
B_ROPE = 64
B_V = 128
B_QK = B_NOPE + B_ROPE
B_Q_RANK = 512
B_KV_RANK = 512
B_WIDTH = B_HEADS * B_V
B_SCALE = B_QK ** -0.5
ROPE_THETA = 10000.0
B_CAT = B_NOPE + 2 * B_ROPE

LANES = 128
VMEM_LIMIT = 56 * 1024 * 1024

F32 = jnp.float32
BF16 = jnp.bfloat16


def _rms_scale(x):
    return lax.rsqrt(jnp.mean(x * x, axis=-1, keepdims=True) + EPS)


def _const_spec(shape):
    return pl.BlockSpec(shape, lambda *_: (0,) * len(shape), pipeline_mode=pl.Buffered(1))


def _proj_a_kernel(x_ref, ln_ref, w_ref, gq_ref, gk_ref, qkvg_ref, k32_ref, v32_ref, h_ref,
                   *, tn, tiles_per_keep, keep_rows):
    i = pl.program_id(0)
    j = pl.program_id(1)
    nseg = A_WIDTH // tn
    tm = x_ref.shape[0]

    @pl.when(j == 0)
    def _():
        x = x_ref[...]
        h_ref[...] = (x * _rms_scale(x) * ln_ref[...]).astype(BF16)

    acc = jnp.dot(h_ref[...], w_ref[...], preferred_element_type=F32)
    seg = j // nseg
    is_tail = (i % tiles_per_keep) == (tiles_per_keep - 1)

    @pl.when(seg == 0)
    def _():
        g = gq_ref[...] * A_SCALE
        for hh in range(tn // LANES):
            s = acc[:, hh * LANES:(hh + 1) * LANES]
            qkvg_ref[:, hh * LANES:(hh + 1) * LANES] = (s * _rms_scale(s) * g).astype(BF16)

    @pl.when(seg == 1)
    def _():
        g = gk_ref[...]
        for hh in range(tn // LANES):
            s = acc[:, hh * LANES:(hh + 1) * LANES]
            kn = s * _rms_scale(s) * g
            qkvg_ref[:, hh * LANES:(hh + 1) * LANES] = kn.astype(BF16)

            @pl.when(is_tail)
            def _():
                k32_ref[:, hh * LANES:(hh + 1) * LANES] = kn[tm - keep_rows:, :]

    @pl.when(seg == 2)
    def _():
        qkvg_ref[...] = acc.astype(BF16)

        @pl.when(is_tail)
        def _():
            v32_ref[...] = acc[tm - keep_rows:, :]

    @pl.when(seg == 3)
    def _():
        qkvg_ref[...] = (acc / (1.0 + jnp.exp(-acc))).astype(BF16)


def _proj_a(x, ln, w, gq, gk, *, tm, tn, keep_rows, tiles_per_keep):
    t = x.shape[0]
    nm = t // tm
    nseg = A_WIDTH // tn
    nkeep = nm // tiles_per_keep

    def tail_col(first_seg):
        def index_map(i, j):
            is_tail = (i % tiles_per_keep) == (tiles_per_keep - 1)
            col = jnp.clip(j - first_seg * nseg, 0, nseg - 1)
            return (i // tiles_per_keep, jnp.where(is_tail, col, 0))
        return index_map

    kernel = functools.partial(_proj_a_kernel, tn=tn, tiles_per_keep=tiles_per_keep, keep_rows=keep_rows)
    return pl.pallas_call(
        kernel,
        grid=(nm, 4 * nseg),
        in_specs=[
            pl.BlockSpec((tm, D_MODEL), lambda i, j: (i, 0)),
            _const_spec((1, D_MODEL)),
            pl.BlockSpec((D_MODEL, tn), lambda i, j: (0, j)),
            _const_spec((1, A_HEAD_DIM)),
            _const_spec((1, A_HEAD_DIM)),
        ],
        out_specs=[
            pl.BlockSpec((tm, tn), lambda i, j: (i, j)),
            pl.BlockSpec((keep_rows, tn), tail_col(1)),
            pl.BlockSpec((keep_rows, tn), tail_col(2)),
        ],
        out_shape=[
            jax.ShapeDtypeStruct((t, 4 * A_WIDTH), BF16),
            jax.ShapeDtypeStruct((nkeep * keep_rows, A_WIDTH), F32),
            jax.ShapeDtypeStruct((nkeep * keep_rows, A_WIDTH), F32),
        ],
        scratch_shapes=[pltpu.VMEM((tm, D_MODEL), BF16)],
        compiler_params=pltpu.CompilerParams(
            dimension_semantics=("arbitrary", "arbitrary"), vmem_limit_bytes=VMEM_LIMIT),
        name="proj_a",
    )(x, ln, w, gq, gk)


def _attn_a_kernel(*refs, has_cache, tq, seq):
    if has_cache:
        q_ref, k_ref, v_ref, kc_ref, vc_ref, bias_ref, o_ref, kpad, vpad = refs
    else:
        q_ref, k_ref, v_ref, bias_ref, o_ref, kpad, vpad = refs
    qi = pl.program_id(2)
    width = A_WINDOW + tq

    @pl.when(qi == 0)
    def _():
        if has_cache:
            kpad[0:A_WINDOW, :] = kc_ref[0].astype(BF16)
            vpad[0:A_WINDOW, :] = vc_ref[0].astype(BF16)
        else:
            kpad[0:A_WINDOW, :] = jnp.zeros((A_WINDOW, A_HEAD_DIM), BF16)
            vpad[0:A_WINDOW, :] = jnp.zeros((A_WINDOW, A_HEAD_DIM), BF16)
        kpad[A_WINDOW:A_WINDOW + seq, :] = k_ref[0]
        vpad[A_WINDOW:A_WINDOW + seq, :] = v_ref[0]

    start = pl.multiple_of(qi * tq, tq)
    kb = kpad[pl.ds(start, width), :]
    vb = vpad[pl.ds(start, width), :]
    s = lax.dot_general(q_ref[0], kb, (((1,), (1,)), ((), ())), preferred_element_type=F32)
    s = s + bias_ref[0]
    if not has_cache:
        col = lax.broadcasted_iota(jnp.int32, (tq, width), 1)
        s = jnp.where(col + start >= A_WINDOW, s, NEG_INF)
    m = jnp.max(s, axis=-1, keepdims=True)
    p = jnp.exp(s - m)
    l = jnp.sum(p, axis=-1, keepdims=True)
    o = jnp.dot(p.astype(BF16), vb, preferred_element_type=F32) / l
    o_ref[0] = o.astype(BF16)


def _attn_a(qkvg, bias, k_cache, v_cache, *, batch, seq, tq):
    has_cache = k_cache is not None
    qkvg3 = qkvg.reshape(batch, seq, 4 * A_WIDTH)
    width = A_WINDOW + tq
    in_specs = [
        pl.BlockSpec((1, tq, A_HEAD_DIM), lambda b, h, qi: (b, qi, h)),
        pl.BlockSpec((1, seq, A_HEAD_DIM), lambda b, h, qi: (b, 0, A_HEADS + h)),
        pl.BlockSpec((1, seq, A_HEAD_DIM), lambda b, h, qi: (b, 0, 2 * A_HEADS + h)),
    ]
    args = [qkvg3, qkvg3, qkvg3]
    if has_cache:
        in_specs += [pl.BlockSpec((1, A_WINDOW, A_HEAD_DIM), lambda b, h, qi: (b, 0, h))] * 2
        args += [k_cache, v_cache]
    in_specs.append(pl.BlockSpec((1, tq, width), lambda b, h, qi: (h, 0, 0)))
    args.append(bias)
    kernel = functools.partial(_attn_a_kernel, has_cache=has_cache, tq=tq, seq=seq)
    out = pl.pallas_call(
        kernel,
        grid=(batch, A_HEADS, seq // tq),
        in_specs=in_specs,
        out_specs=pl.BlockSpec((1, tq, A_HEAD_DIM), lambda b, h, qi: (b, qi, h)),
        out_shape=jax.ShapeDtypeStruct((batch, seq, A_WIDTH), BF16),
        scratch_shapes=[pltpu.VMEM((A_WINDOW + seq, A_HEAD_DIM), BF16),
                        pltpu.VMEM((A_WINDOW + seq, A_HEAD_DIM), BF16)],
        compiler_params=pltpu.CompilerParams(
            dimension_semantics=("arbitrary", "arbitrary", "arbitrary"), vmem_limit_bytes=VMEM_LIMIT),
        name="attn_a",
    )(*args)
    return out.reshape(batch * seq, A_WIDTH)


def _band_bias(table, tq):
    q_pos = jnp.arange(CHUNK) + A_WINDOW
    k_pos = jnp.arange(A_BAND)
    idx = jnp.clip(q_pos[:, None] - k_pos[None, :], -REL_CLIP, REL_CLIP) + REL_CLIP
    band = jnp.take(table, idx, axis=1).astype(F32)
    rows = []
    for c in range(tq // CHUNK):
        left = c * CHUNK
        right = tq - CHUNK - left
        rows.append(jnp.pad(band, ((0, 0), (0, 0), (left, right)), constant_values=NEG_INF))
    return jnp.concatenate(rows, axis=1)


def _out_proj_kernel(x_ref, sg_ref, o_ref, w_ref, y_ref):
    gated = sg_ref[...] * o_ref[...]
    y_ref[...] = x_ref[...] + jnp.dot(gated, w_ref[...], preferred_element_type=F32)


def _out_proj(x, sg_arr, sg_col, o, w, *, tm):
    t, width = o.shape
    return pl.pallas_call(
        _out_proj_kernel,
        grid=(t // tm,),
        in_specs=[
            pl.BlockSpec((tm, D_MODEL), lambda i: (i, 0)),
            pl.BlockSpec((tm, width), lambda i: (i, sg_col)),
            pl.BlockSpec((tm, width), lambda i: (i, 0)),
            _const_spec((width, D_MODEL)),
        ],
        out_specs=pl.BlockSpec((tm, D_MODEL), lambda i: (i, 0)),
        out_shape=jax.ShapeDtypeStruct((t, D_MODEL), F32),
        compiler_params=pltpu.CompilerParams(
            dimension_semantics=("arbitrary",), vmem_limit_bytes=VMEM_LIMIT),
        name="out_proj",
    )(x, sg_arr, o, w)


def _proj_b_kernel(x_ref, ln_ref, w1_ref, wg_ref, wuq_ref, gqa_ref, gkva_ref, gq_ref, gkr_ref, cs_ref,
                   qcat_ref, ckv_ref, kr2_ref, sg_ref):
    x = x_ref[...]
    h = (x * _rms_scale(x) * ln_ref[...]).astype(BF16)
    t1 = jnp.dot(h, w1_ref[...], preferred_element_type=F32)
    cq = t1[:, :B_Q_RANK]
    ckv = t1[:, B_Q_RANK:B_Q_RANK + B_KV_RANK]
    kr = t1[:, B_Q_RANK + B_KV_RANK:]
    cs = cs_ref[...]

    ckv_ref[...] = ckv * _rms_scale(ckv) * gkva_ref[...]
    wk = kr * _rms_scale(kr) * gkr_ref[...] * cs
    kr2_ref[...] = wk + pltpu.roll(wk, B_ROPE, axis=1)

    cqn = (cq * _rms_scale(cq) * gqa_ref[...]).astype(BF16)
    q = jnp.dot(cqn, wuq_ref[...], preferred_element_type=F32)
    g_nope = gq_ref[:, :B_NOPE] * B_SCALE
    g_rope = gq_ref[:, B_NOPE:] * cs * B_SCALE
    for hh in range(B_HEADS):
        qn = q[:, hh * B_CAT:hh * B_CAT + B_NOPE]
        qr = q[:, hh * B_CAT + B_NOPE:(hh + 1) * B_CAT]
        qcat_ref[:, hh * B_CAT:hh * B_CAT + B_NOPE] = (qn * _rms_scale(qn) * g_nope).astype(BF16)
        qcat_ref[:, hh * B_CAT + B_NOPE:(hh + 1) * B_CAT] = (qr * _rms_scale(qr) * g_rope).astype(BF16)

    g = jnp.dot(h, wg_ref[...], preferred_element_type=F32)
    sg_ref[...] = (g / (1.0 + jnp.exp(-g))).astype(BF16)


def _proj_b(x, ln, w1, wg, wuq, gqa, gkva, gq, gkr, cs, *, tm):
    t = x.shape[0]
    n_pos_tiles = cs.shape[0] // tm
    return pl.pallas_call(
        _proj_b_kernel,
        grid=(t // tm,),
        in_specs=[
            pl.BlockSpec((tm, D_MODEL), lambda i: (i, 0)),
            _const_spec((1, D_MODEL)),
            _const_spec(w1.shape),
            _const_spec(wg.shape),
            _const_spec(wuq.shape),
            _const_spec((1, B_Q_RANK)),
            _const_spec((1, B_KV_RANK)),
            _const_spec((1, B_CAT)),
            _const_spec((1, 2 * B_ROPE)),
            pl.BlockSpec((tm, 2 * B_ROPE), lambda i: (i % n_pos_tiles, 0)),
        ],
        out_specs=[
            pl.BlockSpec((tm, B_HEADS * B_CAT), lambda i: (i, 0)),
            pl.BlockSpec((tm, B_KV_RANK), lambda i: (i, 0)),
            pl.BlockSpec((tm, 2 * B_ROPE), lambda i: (i, 0)),
            pl.BlockSpec((tm, B_WIDTH), lambda i: (i, 0)),
        ],
        out_shape=[
            jax.ShapeDtypeStruct((t, B_HEADS * B_CAT), BF16),
            jax.ShapeDtypeStruct((t, B_KV_RANK), F32),
            jax.ShapeDtypeStruct((t, 2 * B_ROPE), F32),
            jax.ShapeDtypeStruct((t, B_WIDTH), BF16),
        ],
        compiler_params=pltpu.CompilerParams(
            dimension_semantics=("arbitrary",), vmem_limit_bytes=VMEM_LIMIT),
        name="proj_b",
    )(x, ln, w1, wg, wuq, gqa, gkva, gq, gkr, cs)


def _expand_b_kernel(ckv_ref, kr2_ref, wuk_ref, wuv_ref, gkn_ref, kcat_ref, v_ref):
    c = ckv_ref[...].astype(BF16)
    kn = jnp.dot(c, wuk_ref[...], preferred_element_type=F32)
    krb = kr2_ref[...].astype(BF16)
    g = gkn_ref[...]
    for hh in range(B_HEADS):
        s = kn[:, hh * B_NOPE:(hh + 1) * B_NOPE]
        kcat_ref[:, hh * B_CAT:hh * B_CAT + B_NOPE] = (s * _rms_scale(s) * g).astype(BF16)
        kcat_ref[:, hh * B_CAT + B_NOPE:(hh + 1) * B_CAT] = krb
    v_ref[...] = jnp.dot(c, wuv_ref[...], preferred_element_type=F32).astype(BF16)


def _expand_b(ckv, kr2, wuk, wuv, gkn, *, tm):
    t = ckv.shape[0]
    return pl.pallas_call(
        _expand_b_kernel,
        grid=(t // tm,),
        in_specs=[
            pl.BlockSpec((tm, B_KV_RANK), lambda i: (i, 0)),
            pl.BlockSpec((tm, 2 * B_ROPE), lambda i: (i, 0)),
            _const_spec(wuk.shape),
            _const_spec(wuv.shape),
            _const_spec((1, B_NOPE)),
        ],
        out_specs=[
            pl.BlockSpec((tm, B_HEADS * B_CAT), lambda i: (i, 0)),
            pl.BlockSpec((tm, B_WIDTH), lambda i: (i, 0)),
        ],
        out_shape=[
            jax.ShapeDtypeStruct((t, B_HEADS * B_CAT), BF16),
            jax.ShapeDtypeStruct((t, B_WIDTH), BF16),
        ],
        compiler_params=pltpu.CompilerParams(
            dimension_semantics=("arbitrary",), vmem_limit_bytes=VMEM_LIMIT),
        name="expand_b",
    )(ckv, kr2, wuk, wuv, gkn)


def _attn_b_kernel(q_ref, k_ref, v_ref, o_ref, *, tq, tk, causal):
    q = q_ref[0]

    def block(k_start, masked, carry):
        m, l, acc = carry
        kb = k_ref[0, pl.ds(k_start, tk), :]
        vb = v_ref[0, pl.ds(k_start, tk), :]
        s = lax.dot_general(q, kb, (((1,), (1,)), ((), ())), preferred_element_type=F32)
        if masked:
            row = lax.broadcasted_iota(jnp.int32, (tq, tk), 0) // CHUNK
            col = lax.broadcasted_iota(jnp.int32, (tq, tk), 1) // CHUNK
            s = jnp.where(col <= row, s, NEG_INF)
        m_new = jnp.maximum(m, jnp.max(s, axis=-1, keepdims=True))
        alpha = jnp.exp(m - m_new)
        p = jnp.exp(s - m_new)
        l = alpha * l + jnp.sum(p, axis=-1, keepdims=True)
        acc = alpha * acc + jnp.dot(p.astype(BF16), vb, preferred_element_type=F32)
        return m_new, l, acc

    carry = (jnp.full((tq, 1), NEG_INF, F32), jnp.zeros((tq, 1), F32), jnp.zeros((tq, B_V), F32))
    if causal:
        qi = pl.program_id(2)
        carry = lax.fori_loop(
            0, qi, lambda kj, c: block(pl.multiple_of(kj * tk, tk), False, c), carry)
        carry = block(pl.multiple_of(qi * tk, tk), True, carry)
    else:
        carry = block(0, False, carry)
    _, l, acc = carry
    o_ref[0] = (acc / l).astype(BF16)


def _attn_b(qcat, kcat, v, *, batch, q_len, k_len, tq, tk, causal):
    q3 = qcat.reshape(batch, q_len, B_HEADS * B_CAT)
    k3 = kcat.reshape(batch, k_len, B_HEADS * B_CAT)
    v3 = v.reshape(batch, k_len, B_WIDTH)
    kernel = functools.partial(_attn_b_kernel, tq=tq, tk=tk, causal=causal)
    out = pl.pallas_call(
        kernel,
        grid=(batch, B_HEADS, q_len // tq),
        in_specs=[
            pl.BlockSpec((1, tq, B_CAT), lambda b, h, qi: (b, qi, h)),
            pl.BlockSpec((1, k_len, B_CAT), lambda b, h, qi: (b, 0, h)),
            pl.BlockSpec((1, k_len, B_V), lambda b, h, qi: (b, 0, h)),
        ],
        out_specs=pl.BlockSpec((1, tq, B_V), lambda b, h, qi: (b, qi, h)),
        out_shape=jax.ShapeDtypeStruct((batch, q_len, B_WIDTH), BF16),
        compiler_params=pltpu.CompilerParams(
            dimension_semantics=("arbitrary", "arbitrary", "arbitrary"), vmem_limit_bytes=VMEM_LIMIT),
        name="attn_b",
    )(q3, k3, v3)
    return out.reshape(batch * q_len, B_WIDTH)


def _swap_halves(a):
    half = a.shape[-1] // 2
    return jnp.concatenate([a[..., half:], a[..., :half]], axis=-1)


def _rope_table(pos):
    half = B_ROPE // 2
    inv = ROPE_THETA ** (-jnp.arange(half, dtype=F32) / half)
    ang = pos.astype(F32)[:, None] * inv[None, :]
    cos, sin = jnp.cos(ang), jnp.sin(ang)
    return jnp.concatenate([cos, cos, -sin, sin], axis=-1)


def _row(v):
    return v.reshape(1, -1).astype(F32)


def _layer_a(x, k_cache, v_cache, ln, w_in, gq, gk, table, w_out, *, batch, seq, tm, tq, tm_out):
    keep = min(A_WINDOW, seq) if k_cache is None else seq
    if k_cache is None:
        keep_rows, tiles_per_keep = keep, seq // tm
    else:
        keep_rows, tiles_per_keep = tm, 1
    qkvg, k32, v32 = _proj_a(x, ln, w_in, gq, gk, tm=tm, tn=512,
                             keep_rows=keep_rows, tiles_per_keep=tiles_per_keep)
    o = _attn_a(qkvg, _band_bias(table, tq), k_cache, v_cache, batch=batch, seq=seq, tq=tq)
    y = _out_proj(x, qkvg, 3, o, w_out, tm=tm_out)
    shape = (batch, keep, A_HEADS, A_HEAD_DIM)
    return y, k32.reshape(shape), v32.reshape(shape)


def _layer_b(x, ckv_cache, kr_cache, pos, ln, w1, wg, wuq, gqa, gkva, gq, gkr, wuk, wuv, gkn, w_out,
             *, batch, seq, tm, tm_expand, tq, tk, tm_out):
    cs = _rope_table(pos)
    if ckv_cache is not None:
        cs = jnp.tile(cs, (batch, 1))
    qcat, ckv, kr2, sg = _proj_b(x, ln, w1, wg, wuq, gqa, gkva, gq, gkr, cs, tm=tm)
    if ckv_cache is None:
        ckv_all, kr2_all, k_len = ckv, kr2, seq
    else:
        past = ckv_cache.shape[1]
        k_len = past + seq
        ckv_all = jnp.concatenate(
            [ckv_cache, ckv.reshape(batch, seq, B_KV_RANK)], axis=1).reshape(batch * k_len, B_KV_RANK)
        kr2_all = jnp.concatenate(
            [jnp.concatenate([kr_cache, kr_cache], axis=-1), kr2.reshape(batch, seq, 2 * B_ROPE)],
            axis=1).reshape(batch * k_len, 2 * B_ROPE)
    kcat, v = _expand_b(ckv_all, kr2_all, wuk, wuv, gkn, tm=tm_expand)
    o = _attn_b(qcat, kcat, v, batch=batch, q_len=seq, k_len=k_len, tq=tq,
                tk=tk if ckv_cache is None else k_len, causal=ckv_cache is None)
    y = _out_proj(x, sg, 0, o, w_out, tm=tm_out)
    return y, ckv.reshape(batch, seq, B_KV_RANK), kr2[:, :B_ROPE].reshape(batch, seq, B_ROPE)


def kernel(x_prompt, x_sample, cache_a_k, cache_a_v, cache_b_ckv, cache_b_krope, a_ln, w_a_in, a_q_norm, a_k_norm, a_rel_bias, w_a_out, b_ln, w_b_in, b_q_a_norm, w_b_uq, b_kv_a_norm, w_b_uk, w_b_uv, b_q_nope_norm, b_k_nope_norm, b_q_rope_norm, b_k_rope_norm, w_b_out):
    batch, seq, _ = x_prompt.shape
    dec_batch, dec_seq, _ = x_sample.shape
    depth = a_ln.shape[0] + b_ln.shape[0]
    yp = x_prompt.reshape(batch * seq, D_MODEL)
    ys = x_sample.reshape(dec_batch * dec_seq, D_MODEL)
    akp, avp, aks, avs, bcp, brp, bcs, brs = ([] for _ in range(8))
    for layer in range(depth):
        i = layer // 2
        if layer % 2 == 0:
            params = (_row(a_ln[i]), w_a_in[i].astype(BF16), _row(a_q_norm[i]), _row(a_k_norm[i]),
                      a_rel_bias[i], w_a_out[i].astype(BF16))
            yp, k_p, v_p = _layer_a(yp, None, None, *params, batch=batch, seq=seq,
                                    tm=1024, tq=2 * CHUNK, tm_out=512)
            kc = cache_a_k[i].reshape(dec_batch, -1, A_WIDTH)
            vc = cache_a_v[i].reshape(dec_batch, -1, A_WIDTH)
            ys, k_s, v_s = _layer_a(ys, kc, vc, *params, batch=dec_batch, seq=dec_seq,
                                    tm=dec_batch * dec_seq, tq=CHUNK, tm_out=dec_batch * dec_seq)
            akp.append(k_p); avp.append(v_p); aks.append(k_s); avs.append(v_s)
        else:
            w_in = w_b_in[i]
            c0, c1, c2 = B_Q_RANK, B_Q_RANK + B_KV_RANK, B_Q_RANK + B_KV_RANK + B_ROPE
            w_kr = w_in[:, c1:c2]
            w1 = jnp.concatenate([w_in[:, :c1], w_kr, _swap_halves(w_kr)], axis=1).astype(BF16)
            wg = w_in[:, c2:].astype(BF16)
            wuq = w_b_uq[i].reshape(B_Q_RANK, B_HEADS, B_QK)
            wuq_rope = wuq[..., B_NOPE:]
            wuq = jnp.concatenate([wuq[..., :B_NOPE], wuq_rope, _swap_halves(wuq_rope)],
                                  axis=-1).reshape(B_Q_RANK, B_HEADS * B_CAT).astype(BF16)
            g_qr, g_kr = b_q_rope_norm[i], b_k_rope_norm[i]
            gq = _row(jnp.concatenate([b_q_nope_norm[i], g_qr, _swap_halves(g_qr)]))
            gkr = _row(jnp.concatenate([g_kr, _swap_halves(g_kr)]))
            params = (_row(b_ln[i]), w1, wg, wuq, _row(b_q_a_norm[i]), _row(b_kv_a_norm[i]), gq, gkr,
                      w_b_uk[i].astype(BF16), w_b_uv[i].astype(BF16), _row(b_k_nope_norm[i]),
                      w_b_out[i].astype(BF16))
            yp, c_p, r_p = _layer_b(yp, None, None, jnp.arange(seq), *params, batch=batch, seq=seq,
                                    tm=256, tm_expand=512, tq=512, tk=512, tm_out=512)
            ys, c_s, r_s = _layer_b(ys, cache_b_ckv[i], cache_b_krope[i], PAST_LEN + jnp.arange(dec_seq),
                                    *params, batch=dec_batch, seq=dec_seq,
                                    tm=256, tm_expand=512, tq=dec_seq, tk=512, tm_out=dec_batch * dec_seq)
            bcp.append(c_p); brp.append(r_p); bcs.append(c_s); brs.append(r_s)
    yp = yp.reshape(batch, seq, D_MODEL)
    ys = ys.reshape(dec_batch, dec_seq, D_MODEL)
    return (yp, ys,
            jnp.stack(akp), jnp.stack(avp), jnp.stack(bcp), jnp.stack(brp),
            jnp.stack(aks), jnp.stack(avs), jnp.stack(bcs), jnp.stack(brs))
```

```python
import functools
import math

import numpy as np

import jax
import jax.numpy as jnp
from jax import lax
from jax.experimental import pallas as pl
from jax.experimental.pallas import tpu as pltpu

D_MODEL = 2048
CHUNK = 64
EPS = 1e-6
NEG_INF = -1e30
PAST_LEN = 1024
LOG2E = math.log2(math.e)

A_HEADS = 16
A_HEAD_DIM = 128
A_WIDTH = A_HEADS * A_HEAD_DIM
A_WINDOW = 8 * CHUNK
A_BAND = A_WINDOW + CHUNK
REL_CLIP = 128
A_SCALE = A_HEAD_DIM ** -0.5

B_HEADS = 16
B_NOPE = 128
B_ROPE = 64
B_V = 128
B_QK = B_NOPE + B_ROPE
B_Q_RANK = 512
B_KV_RANK = 512
B_WIDTH = B_HEADS * B_V
B_SCALE = B_QK ** -0.5
ROPE_THETA = 10000.0
B_CAT = B_NOPE + 2 * B_ROPE
BF16_SUBLANES = 16
VT_ROWS = B_V + BF16_SUBLANES

LANES = 128
VMEM_LIMIT = 56 * 1024 * 1024

F32 = jnp.float32
BF16 = jnp.bfloat16
_NT = (((1,), (1,)), ((), ()))


def _rms_scale(x):
    return lax.rsqrt(jnp.mean(x * x, axis=-1, keepdims=True) + EPS)


def _const_spec(shape):
    return pl.BlockSpec(shape, lambda *_: (0,) * len(shape), pipeline_mode=pl.Buffered(1))


def _proj_a_kernel(x_ref, ln_ref, w_ref, gq_ref, gk_ref, qkvg_ref, k32_ref, v32_ref, h_ref,
                   *, tn, tiles_per_keep, keep_rows):
    i = pl.program_id(0)
    j = pl.program_id(1)
    nseg = A_WIDTH // tn
    tm = x_ref.shape[0]

    @pl.when(j == 0)
    def _():
        x = x_ref[...]
        h_ref[...] = (x * _rms_scale(x) * ln_ref[...]).astype(BF16)

    acc = jnp.dot(h_ref[...], w_ref[...], preferred_element_type=F32)
    seg = j // nseg
    is_tail = (i % tiles_per_keep) == (tiles_per_keep - 1)

    @pl.when(seg == 0)
    def _():
        g = gq_ref[...] * (A_SCALE * LOG2E)
        for hh in range(tn // LANES):
            s = acc[:, hh * LANES:(hh + 1) * LANES]
            qkvg_ref[:, hh * LANES:(hh + 1) * LANES] = (s * _rms_scale(s) * g).astype(BF16)

    @pl.when(seg == 1)
    def _():
        g = gk_ref[...]
        for hh in range(tn // LANES):
            s = acc[:, hh * LANES:(hh + 1) * LANES]
            kn = s * _rms_scale(s) * g
            qkvg_ref[:, hh * LANES:(hh + 1) * LANES] = kn.astype(BF16)

            @pl.when(is_tail)
            def _():
                k32_ref[:, hh * LANES:(hh + 1) * LANES] = kn[tm - keep_rows:, :]

    @pl.when(seg == 2)
    def _():
        qkvg_ref[...] = acc.astype(BF16)

        @pl.when(is_tail)
        def _():
            v32_ref[...] = acc[tm - keep_rows:, :]

    @pl.when(seg == 3)
    def _():
        qkvg_ref[...] = (acc / (1.0 + jnp.exp(-acc))).astype(BF16)


def _proj_a(x, ln, w, gq, gk, *, tm, tn, keep_rows, tiles_per_keep):
    t = x.shape[0]
    nm = t // tm
    nseg = A_WIDTH // tn
    nkeep = nm // tiles_per_keep

    def tail_col(first_seg):
        def index_map(i, j):
            is_tail = (i % tiles_per_keep) == (tiles_per_keep - 1)
            col = jnp.clip(j - first_seg * nseg, 0, nseg - 1)
            return (i // tiles_per_keep, jnp.where(is_tail, col, 0))
        return index_map

    kernel = functools.partial(_proj_a_kernel, tn=tn, tiles_per_keep=tiles_per_keep, keep_rows=keep_rows)
    return pl.pallas_call(
        kernel,
        grid=(nm, 4 * nseg),
        in_specs=[
            pl.BlockSpec((tm, D_MODEL), lambda i, j: (i, 0)),
            _const_spec((1, D_MODEL)),
            pl.BlockSpec((D_MODEL, tn), lambda i, j: (0, j)),
            _const_spec((1, A_HEAD_DIM)),
            _const_spec((1, A_HEAD_DIM)),
        ],
        out_specs=[
            pl.BlockSpec((tm, tn), lambda i, j: (i, j)),
            pl.BlockSpec((keep_rows, tn), tail_col(1)),
            pl.BlockSpec((keep_rows, tn), tail_col(2)),
        ],
        out_shape=[
            jax.ShapeDtypeStruct((t, 4 * A_WIDTH), BF16),
            jax.ShapeDtypeStruct((nkeep * keep_rows, A_WIDTH), F32),
            jax.ShapeDtypeStruct((nkeep * keep_rows, A_WIDTH), F32),
        ],
        scratch_shapes=[pltpu.VMEM((tm, D_MODEL), BF16)],
        compiler_params=pltpu.CompilerParams(
            dimension_semantics=("arbitrary", "arbitrary"), vmem_limit_bytes=VMEM_LIMIT),
        name="proj_a",
    )(x, ln, w, gq, gk)


def _attn_a_kernel(*refs, has_cache, tq, seq, n_bias):
    if has_cache:
        q_ref, k_ref, v_ref, kc_ref, vc_ref, bias_ref, o_ref, kpad, vpad, s_ref = refs
    else:
        q_ref, k_ref, v_ref, bias_ref, o_ref, kpad, vpad, s_ref = refs
    width = A_WINDOW + tq
    n_blocks = seq // tq

    if has_cache:
        kpad[0:A_WINDOW, :] = kc_ref[0].astype(BF16)
        vpad[0:A_WINDOW, :] = vc_ref[0].astype(BF16)
    else:
        kpad[0:A_WINDOW, :] = jnp.zeros((A_WINDOW, A_HEAD_DIM), BF16)
        vpad[0:A_WINDOW, :] = jnp.zeros((A_WINDOW, A_HEAD_DIM), BF16)
    kpad[A_WINDOW:A_WINDOW + seq, :] = k_ref[0]
    vpad[A_WINDOW:A_WINDOW + seq, :] = v_ref[0]

    def scores(t):
        q = q_ref[0, t * tq:(t + 1) * tq, :]
        kb = kpad[t * tq:t * tq + width, :]
        s_ref[t % 2] = lax.dot_general(q, kb, _NT, preferred_element_type=F32)

    scores(0)
    for t in range(n_blocks):
        if t + 1 < n_blocks:
            scores(t + 1)
        s = s_ref[t % 2] + bias_ref[0, min(t, n_bias - 1)]
        m = jnp.max(s, axis=-1, keepdims=True)
        p = jnp.exp2(s - m)
        l = jnp.sum(p, axis=-1, keepdims=True)
        vb = vpad[t * tq:t * tq + width, :]
        o = jnp.dot(p.astype(BF16), vb, preferred_element_type=F32) / l
        o_ref[0, t * tq:(t + 1) * tq, :] = o.astype(BF16)


def _attn_a(qkvg, bias, k_cache, v_cache, *, batch, seq, tq):
    has_cache = k_cache is not None
    qkvg3 = qkvg.reshape(batch, seq, 4 * A_WIDTH)
    width = A_WINDOW + tq
    n_bias = bias.shape[1]
    in_specs = [
        pl.BlockSpec((1, seq, A_HEAD_DIM), lambda b, h: (b, 0, h)),
        pl.BlockSpec((1, seq, A_HEAD_DIM), lambda b, h: (b, 0, A_HEADS + h)),
        pl.BlockSpec((1, seq, A_HEAD_DIM), lambda b, h: (b, 0, 2 * A_HEADS + h)),
    ]
    args = [qkvg3, qkvg3, qkvg3]
    if has_cache:
        in_specs += [pl.BlockSpec((1, A_WINDOW, A_HEAD_DIM), lambda b, h: (b, 0, h))] * 2
        args += [k_cache, v_cache]
    in_specs.append(pl.BlockSpec((1, n_bias, tq, width), lambda b, h: (h, 0, 0, 0)))
    args.append(bias)
    kernel = functools.partial(_attn_a_kernel, has_cache=has_cache, tq=tq, seq=seq, n_bias=n_bias)
    out = pl.pallas_call(
        kernel,
        grid=(batch, A_HEADS),
        in_specs=in_specs,
        out_specs=pl.BlockSpec((1, seq, A_HEAD_DIM), lambda b, h: (b, 0, h)),
        out_shape=jax.ShapeDtypeStruct((batch, seq, A_WIDTH), BF16),
        scratch_shapes=[pltpu.VMEM((A_WINDOW + seq, A_HEAD_DIM), BF16),
                        pltpu.VMEM((A_WINDOW + seq, A_HEAD_DIM), BF16),
                        pltpu.VMEM((2, tq, width), F32)],
        compiler_params=pltpu.CompilerParams(
            dimension_semantics=("arbitrary", "arbitrary"), vmem_limit_bytes=VMEM_LIMIT),
        name="attn_a",
    )(*args)
    return out.reshape(batch * seq, A_WIDTH)


def _band_bias(table, tq, mask_prefix):
    width = A_WINDOW + tq
    length = tq + width - 1
    n_table = 2 * REL_CLIP + 1
    c0 = A_WINDOW + REL_CLIP - (width - 1)
    n_lo, first = max(0, -c0), max(0, c0)
    n_hi = length - n_lo - (n_table - first)
    heads = table.shape[0]
    v = jnp.concatenate([jnp.broadcast_to(table[:, :1], (heads, n_lo)), table[:, first:],
                         jnp.broadcast_to(table[:, -1:], (heads, n_hi))], axis=1).astype(F32)
    w = jnp.pad(v[:, ::-1], ((0, 0), (0, 1)))
    skew = jnp.tile(w, (1, tq))[:, :tq * length].reshape(heads, tq, length)
    toeplitz = skew[:, :, tq - 1:tq - 1 + width] * LOG2E

    r = np.arange(tq)[:, None]
    j = np.arange(width)[None, :]
    in_band = (j >= CHUNK * (r // CHUNK)) & (j < CHUNK * (r // CHUNK) + A_BAND)
    n = A_WINDOW // tq + 1 if mask_prefix else 1
    keep = np.stack([in_band & (j + v_idx * tq >= A_WINDOW if mask_prefix else True) for v_idx in range(n)])
    return jnp.where(keep[None], toeplitz[:, None], NEG_INF)


def _out_proj_kernel(x_ref, sg_ref, o_ref, w_ref, y_ref):
    gated = sg_ref[...] * o_ref[...]
    y_ref[...] = x_ref[...] + jnp.dot(gated, w_ref[...], preferred_element_type=F32)


def _out_proj(x, sg_arr, sg_col, o, w, *, tm):
    t, width = o.shape
    return pl.pallas_call(
        _out_proj_kernel,
        grid=(t // tm,),
        in_specs=[
            pl.BlockSpec((tm, D_MODEL), lambda i: (i, 0)),
            pl.BlockSpec((tm, width), lambda i: (i, sg_col)),
            pl.BlockSpec((tm, width), lambda i: (i, 0)),
            _const_spec((width, D_MODEL)),
        ],
        out_specs=pl.BlockSpec((tm, D_MODEL), lambda i: (i, 0)),
        out_shape=jax.ShapeDtypeStruct((t, D_MODEL), F32),
        compiler_params=pltpu.CompilerParams(
            dimension_semantics=("arbitrary",), vmem_limit_bytes=VMEM_LIMIT),
        name="out_proj",
    )(x, sg_arr, o, w)


def _proj_b_kernel(x_ref, ln_ref, w1_ref, wg_ref, wuq_ref, gqa_ref, gkva_ref, gq_ref, gkr_ref, cs_ref,
                   qcat_ref, ckv_ref, kr2_ref, sg_ref):
    x = x_ref[...]
    h = (x * _rms_scale(x) * ln_ref[...]).astype(BF16)
    t1 = jnp.dot(h, w1_ref[...], preferred_element_type=F32)
    cq = t1[:, :B_Q_RANK]
    ckv = t1[:, B_Q_RANK:B_Q_RANK + B_KV_RANK]
    kr = t1[:, B_Q_RANK + B_KV_RANK:]
    cs = cs_ref[...]

    ckv_ref[...] = ckv * _rms_scale(ckv) * gkva_ref[...]
    wk = kr * _rms_scale(kr) * gkr_ref[...] * cs
    kr2_ref[...] = wk + pltpu.roll(wk, B_ROPE, axis=1)

    cqn = (cq * _rms_scale(cq) * gqa_ref[...]).astype(BF16)
    q = jnp.dot(cqn, wuq_ref[...], preferred_element_type=F32)
    g_nope = gq_ref[:, :B_NOPE] * (B_SCALE * LOG2E)
    g_rope = gq_ref[:, B_NOPE:] * cs * (B_SCALE * LOG2E)
    for hh in range(B_HEADS):
        qn = q[:, hh * B_CAT:hh * B_CAT + B_NOPE]
        qr = q[:, hh * B_CAT + B_NOPE:(hh + 1) * B_CAT]
        qcat_ref[:, hh * B_CAT:hh * B_CAT + B_NOPE] = (qn * _rms_scale(qn) * g_nope).astype(BF16)
        qcat_ref[:, hh * B_CAT + B_NOPE:(hh + 1) * B_CAT] = (qr * _rms_scale(qr) * g_rope).astype(BF16)

    g = jnp.dot(h, wg_ref[...], preferred_element_type=F32)
    sg_ref[...] = (g / (1.0 + jnp.exp(-g))).astype(BF16)


def _proj_b(x, ln, w1, wg, wuq, gqa, gkva, gq, gkr, cs, *, tm):
    t = x.shape[0]
    n_pos_tiles = cs.shape[0] // tm
    return pl.pallas_call(
        _proj_b_kernel,
        grid=(t // tm,),
        in_specs=[
            pl.BlockSpec((tm, D_MODEL), lambda i: (i, 0)),
            _const_spec((1, D_MODEL)),
            _const_spec(w1.shape),
            _const_spec(wg.shape),
            _const_spec(wuq.shape),
            _const_spec((1, B_Q_RANK)),
            _const_spec((1, B_KV_RANK)),
            _const_spec((1, B_CAT)),
            _const_spec((1, 2 * B_ROPE)),
            pl.BlockSpec((tm, 2 * B_ROPE), lambda i: (i % n_pos_tiles, 0)),
        ],
        out_specs=[
            pl.BlockSpec((tm, B_HEADS * B_CAT), lambda i: (i, 0)),
            pl.BlockSpec((tm, B_KV_RANK), lambda i: (i, 0)),
            pl.BlockSpec((tm, 2 * B_ROPE), lambda i: (i, 0)),
            pl.BlockSpec((tm, B_WIDTH), lambda i: (i, 0)),
        ],
        out_shape=[
            jax.ShapeDtypeStruct((t, B_HEADS * B_CAT), BF16),
            jax.ShapeDtypeStruct((t, B_KV_RANK), F32),
            jax.ShapeDtypeStruct((t, 2 * B_ROPE), F32),
            jax.ShapeDtypeStruct((t, B_WIDTH), BF16),
        ],
        compiler_params=pltpu.CompilerParams(
            dimension_semantics=("arbitrary",), vmem_limit_bytes=VMEM_LIMIT),
        name="proj_b",
    )(x, ln, w1, wg, wuq, gqa, gkva, gq, gkr, cs)


def _expand_b_kernel(ckv_ref, kr2_ref, wuk_ref, wuv_ref, gkn_ref, kcat_ref, v_ref, *, transposed_v):
    c = ckv_ref[...].astype(BF16)
    kn = jnp.dot(c, wuk_ref[...], preferred_element_type=F32)
    krb = kr2_ref[...].astype(BF16)
    g = gkn_ref[...]
    for hh in range(B_HEADS):
        s = kn[:, hh * B_NOPE:(hh + 1) * B_NOPE]
        kcat_ref[:, hh * B_CAT:hh * B_CAT + B_NOPE] = (s * _rms_scale(s) * g).astype(BF16)
        kcat_ref[:, hh * B_CAT + B_NOPE:(hh + 1) * B_CAT] = krb
    if transposed_v:
        vt = lax.dot_general(wuv_ref[...], c, _NT, preferred_element_type=F32).astype(BF16)
        tm = c.shape[0]
        for hh in range(B_HEADS):
            v_ref[0, hh, 0, 0:B_V, :] = vt[hh * B_V:(hh + 1) * B_V, :]
            v_ref[0, hh, 0, B_V:VT_ROWS, :] = jnp.ones((VT_ROWS - B_V, tm), BF16)
    else:
        v_ref[...] = jnp.dot(c, wuv_ref[...], preferred_element_type=F32).astype(BF16)


def _expand_b(ckv, kr2, wuk, wuv, gkn, *, tm, batch=None):
    t = ckv.shape[0]
    transposed_v = batch is not None
    if transposed_v:
        tiles = t // batch // tm
        v_spec = pl.BlockSpec((1, B_HEADS, 1, VT_ROWS, tm), lambda i: (i // tiles, 0, i % tiles, 0, 0))
        v_shape = jax.ShapeDtypeStruct((batch, B_HEADS, tiles, VT_ROWS, tm), BF16)
    else:
        v_spec = pl.BlockSpec((tm, B_WIDTH), lambda i: (i, 0))
        v_shape = jax.ShapeDtypeStruct((t, B_WIDTH), BF16)
    return pl.pallas_call(
        functools.partial(_expand_b_kernel, transposed_v=transposed_v),
        grid=(t // tm,),
        in_specs=[
            pl.BlockSpec((tm, B_KV_RANK), lambda i: (i, 0)),
            pl.BlockSpec((tm, 2 * B_ROPE), lambda i: (i, 0)),
            _const_spec(wuk.shape),
            _const_spec(wuv.shape),
            _const_spec((1, B_NOPE)),
        ],
        out_specs=[pl.BlockSpec((tm, B_HEADS * B_CAT), lambda i: (i, 0)), v_spec],
        out_shape=[jax.ShapeDtypeStruct((t, B_HEADS * B_CAT), BF16), v_shape],
        compiler_params=pltpu.CompilerParams(
            dimension_semantics=("arbitrary",), vmem_limit_bytes=VMEM_LIMIT),
        name="expand_b",
    )(ckv, kr2, wuk, wuv, gkn)


def _attn_bt_kernel(q_ref, k_ref, vt_ref, o_ref, s_ref, mask_ref, m_ref, acc_ref, *, tq, n_blocks):
    pairs = [(qi, kj) for qi in range(n_blocks) for kj in range(qi + 1)]

    key_chunk = lax.broadcasted_iota(jnp.int32, (tq, tq), 0) // CHUNK
    q_chunk = lax.broadcasted_iota(jnp.int32, (tq, tq), 1) // CHUNK
    mask_ref[...] = jnp.where(key_chunk <= q_chunk, 0.0, NEG_INF).astype(F32)

    def scores(t):
        qi, kj = pairs[t]
        kb = k_ref[0, kj * tq:(kj + 1) * tq, :]
        q = q_ref[0, qi * tq:(qi + 1) * tq, :]
        s_ref[t % 2] = lax.dot_general(kb, q, _NT, preferred_element_type=F32)

    scores(0)
    for t, (qi, kj) in enumerate(pairs):
        if t + 1 < len(pairs):
            scores(t + 1)
        s = s_ref[t % 2]
        if kj == qi:
            s = s + mask_ref[...]
        col_max = jnp.max(s, axis=0, keepdims=True)
        m_new = col_max if kj == 0 else jnp.maximum(m_ref[...], col_max)
        p = jnp.exp2(s - m_new).astype(BF16)
        pv = jnp.dot(vt_ref[0, 0, kj], p, preferred_element_type=F32)
        acc = pv if kj == 0 else jnp.exp2(m_ref[...] - m_new) * acc_ref[...] + pv
        if kj == qi:
            o_t = acc[:B_V, :] / acc[B_V:B_V + 1, :]
            o_ref[0, qi * tq:(qi + 1) * tq, :] = o_t.T.astype(BF16)
        else:
            acc_ref[...] = acc
            m_ref[...] = m_new


def _attn_bt(qcat, kcat, vt, *, batch, seq, tq):
    q3 = qcat.reshape(batch, seq, B_HEADS * B_CAT)
    k3 = kcat.reshape(batch, seq, B_HEADS * B_CAT)
    n_blocks = seq // tq
    out = pl.pallas_call(
        functools.partial(_attn_bt_kernel, tq=tq, n_blocks=n_blocks),
        grid=(batch, B_HEADS),
        in_specs=[
            pl.BlockSpec((1, seq, B_CAT), lambda b, h: (b, 0, h)),
            pl.BlockSpec((1, seq, B_CAT), lambda b, h: (b, 0, h)),
            pl.BlockSpec((1, 1, n_blocks, VT_ROWS, tq), lambda b, h: (b, h, 0, 0, 0)),
        ],
        out_specs=pl.BlockSpec((1, seq, B_V), lambda b, h: (b, 0, h)),
        out_shape=jax.ShapeDtypeStruct((batch, seq, B_WIDTH), BF16),
        scratch_shapes=[pltpu.VMEM((2, tq, tq), F32), pltpu.VMEM((tq, tq), F32),
                        pltpu.VMEM((1, tq), F32), pltpu.VMEM((VT_ROWS, tq), F32)],
        compiler_params=pltpu.CompilerParams(
            dimension_semantics=("arbitrary", "arbitrary"), vmem_limit_bytes=VMEM_LIMIT),
        name="attn_bt",
    )(q3, k3, vt)
    return out.reshape(batch * seq, B_WIDTH)


def _attn_b_kernel(q_ref, k_ref, v_ref, o_ref):
    s = lax.dot_general(q_ref[0], k_ref[0], _NT, preferred_element_type=F32)
    p = jnp.exp2(s - jnp.max(s, axis=-1, keepdims=True))
    l = jnp.sum(p, axis=-1, keepdims=True)
    o_ref[0] = (jnp.dot(p.astype(BF16), v_ref[0], preferred_element_type=F32) / l).astype(BF16)


def _attn_b(qcat, kcat, v, *, batch, q_len, k_len):
    q3 = qcat.reshape(batch, q_len, B_HEADS * B_CAT)
    k3 = kcat.reshape(batch, k_len, B_HEADS * B_CAT)
    v3 = v.reshape(batch, k_len, B_WIDTH)
    out = pl.pallas_call(
        _attn_b_kernel,
        grid=(batch, B_HEADS),
        in_specs=[
            pl.BlockSpec((1, q_len, B_CAT), lambda b, h: (b, 0, h)),
            pl.BlockSpec((1, k_len, B_CAT), lambda b, h: (b, 0, h)),
            pl.BlockSpec((1, k_len, B_V), lambda b, h: (b, 0, h)),
        ],
        out_specs=pl.BlockSpec((1, q_len, B_V), lambda b, h: (b, 0, h)),
        out_shape=jax.ShapeDtypeStruct((batch, q_len, B_WIDTH), BF16),
        compiler_params=pltpu.CompilerParams(
            dimension_semantics=("arbitrary", "arbitrary"), vmem_limit_bytes=VMEM_LIMIT),
        name="attn_b",
    )(q3, k3, v3)
    return out.reshape(batch * q_len, B_WIDTH)


def _swap_halves(a):
    half = a.shape[-1] // 2
    return jnp.concatenate([a[..., half:], a[..., :half]], axis=-1)


def _rope_table(pos):
    half = B_ROPE // 2
    inv = ROPE_THETA ** (-jnp.arange(half, dtype=F32) / half)
    ang = pos.astype(F32)[:, None] * inv[None, :]
    cos, sin = jnp.cos(ang), jnp.sin(ang)
    return jnp.concatenate([cos, cos, -sin, sin], axis=-1)


def _row(v):
    return v.reshape(1, -1).astype(F32)


def _layer_a(x, k_cache, v_cache, ln, w_in, gq, gk, table, w_out, *, batch, seq, tm, tq, tm_out):
    if k_cache is None:
        keep = min(A_WINDOW, seq)
        keep_rows, tiles_per_keep = keep, seq // tm
    else:
        keep = seq
        keep_rows, tiles_per_keep = tm, 1
    qkvg, k32, v32 = _proj_a(x, ln, w_in, gq, gk, tm=tm, tn=512,
                             keep_rows=keep_rows, tiles_per_keep=tiles_per_keep)
    bias = _band_bias(table, tq, mask_prefix=k_cache is None)
    o = _attn_a(qkvg, bias, k_cache, v_cache, batch=batch, seq=seq, tq=tq)
    y = _out_proj(x, qkvg, 3, o, w_out, tm=tm_out)
    shape = (batch, keep, A_HEADS, A_HEAD_DIM)
    return y, k32.reshape(shape), v32.reshape(shape)


def _layer_b(x, ckv_cache, kr_cache, pos, ln, w1, wg, wuq, gqa, gkva, gq, gkr, wuk, wuv, gkn, w_out,
             *, batch, seq, tm, tm_out):
    cs = _rope_table(pos)
    if ckv_cache is not None:
        cs = jnp.tile(cs, (batch, 1))
    qcat, ckv, kr2, sg = _proj_b(x, ln, w1, wg, wuq, gqa, gkva, gq, gkr, cs, tm=tm)
    if ckv_cache is None:
        tq = 512
        kcat, vt = _expand_b(ckv, kr2, wuk, wuv.T, gkn, tm=tq, batch=batch)
        o = _attn_bt(qcat, kcat, vt, batch=batch, seq=seq, tq=tq)
    else:
        k_len = ckv_cache.shape[1] + seq
        ckv_all = jnp.concatenate(
            [ckv_cache, ckv.reshape(batch, seq, B_KV_RANK)], axis=1).reshape(batch * k_len, B_KV_RANK)
        kr2_all = jnp.concatenate(
            [jnp.concatenate([kr_cache, kr_cache], axis=-1), kr2.reshape(batch, seq, 2 * B_ROPE)],
            axis=1).reshape(batch * k_len, 2 * B_ROPE)
        kcat, v = _expand_b(ckv_all, kr2_all, wuk, wuv, gkn, tm=512)
        o = _attn_b(qcat, kcat, v, batch=batch, q_len=seq, k_len=k_len)
    y = _out_proj(x, sg, 0, o, w_out, tm=tm_out)
    return y, ckv.reshape(batch, seq, B_KV_RANK), kr2[:, :B_ROPE].reshape(batch, seq, B_ROPE)


def kernel(x_prompt, x_sample, cache_a_k, cache_a_v, cache_b_ckv, cache_b_krope, a_ln, w_a_in, a_q_norm, a_k_norm, a_rel_bias, w_a_out, b_ln, w_b_in, b_q_a_norm, w_b_uq, b_kv_a_norm, w_b_uk, w_b_uv, b_q_nope_norm, b_k_nope_norm, b_q_rope_norm, b_k_rope_norm, w_b_out):
    batch, seq, _ = x_prompt.shape
    dec_batch, dec_seq, _ = x_sample.shape
    dec_tokens = dec_batch * dec_seq
    depth = a_ln.shape[0] + b_ln.shape[0]
    yp = x_prompt.reshape(batch * seq, D_MODEL)
    ys = x_sample.reshape(dec_tokens, D_MODEL)
    akp, avp, aks, avs, bcp, brp, bcs, brs = ([] for _ in range(8))
    for layer in range(depth):
        i = layer // 2
        if layer % 2 == 0:
            params = (_row(a_ln[i]), w_a_in[i].astype(BF16), _row(a_q_norm[i]), _row(a_k_norm[i]),
                      a_rel_bias[i], w_a_out[i].astype(BF16))
            yp, k_p, v_p = _layer_a(yp, None, None, *params, batch=batch, seq=seq,
                                    tm=1024, tq=4 * CHUNK, tm_out=512)
            kc = cache_a_k[i].reshape(dec_batch, -1, A_WIDTH)
            vc = cache_a_v[i].reshape(dec_batch, -1, A_WIDTH)
            ys, k_s, v_s = _layer_a(ys, kc, vc, *params, batch=dec_batch, seq=dec_seq,
                                    tm=dec_tokens, tq=CHUNK, tm_out=dec_tokens)
            akp.append(k_p); avp.append(v_p); aks.append(k_s); avs.append(v_s)
        else:
            w_in = w_b_in[i]
            c1, c2 = B_Q_RANK + B_KV_RANK, B_Q_RANK + B_KV_RANK + B_ROPE
            w_kr = w_in[:, c1:c2]
            w1 = jnp.concatenate([w_in[:, :c1], w_kr, _swap_halves(w_kr)], axis=1).astype(BF16)
            wg = w_in[:, c2:].astype(BF16)
            wuq = w_b_uq[i].reshape(B_Q_RANK, B_HEADS, B_QK)
            wuq_rope = wuq[..., B_NOPE:]
            wuq = jnp.concatenate([wuq[..., :B_NOPE], wuq_rope, _swap_halves(wuq_rope)],
                                  axis=-1).reshape(B_Q_RANK, B_HEADS * B_CAT).astype(BF16)
            g_qr, g_kr = b_q_rope_norm[i], b_k_rope_norm[i]
            gq = _row(jnp.concatenate([b_q_nope_norm[i], g_qr, _swap_halves(g_qr)]))
            gkr = _row(jnp.concatenate([g_kr, _swap_halves(g_kr)]))
            params = (_row(b_ln[i]), w1, wg, wuq, _row(b_q_a_norm[i]), _row(b_kv_a_norm[i]), gq, gkr,
                      w_b_uk[i].astype(BF16), w_b_uv[i].astype(BF16), _row(b_k_nope_norm[i]),
                      w_b_out[i].astype(BF16))
            yp, c_p, r_p = _layer_b(yp, None, None, jnp.arange(seq), *params, batch=batch, seq=seq,
                                    tm=256, tm_out=512)
            ys, c_s, r_s = _layer_b(ys, cache_b_ckv[i], cache_b_krope[i], PAST_LEN + jnp.arange(dec_seq),
                                    *params, batch=dec_batch, seq=dec_seq, tm=256, tm_out=dec_tokens)
            bcp.append(c_p); brp.append(r_p); bcs.append(c_s); brs.append(r_s)
    yp = yp.reshape(batch, seq, D_MODEL)
    ys = ys.reshape(dec_batch, dec_seq, D_MODEL)
    return (yp, ys,
            jnp.stack(akp), jnp.stack(avp), jnp.stack(bcp), jnp.stack(brp),
            jnp.stack(aks), jnp.stack(avs), jnp.stack(bcs), jnp.stack(brs))
```

```python
import functools
import math

import numpy as np

import jax
import jax.numpy as jnp
from jax import lax
from jax.experimental import pallas as pl
from jax.experimental.pallas import tpu as pltpu

D_MODEL = 2048
CHUNK = 64
EPS = 1e-6
NEG_INF = -1e30
PAST_LEN = 1024
LOG2E = math.log2(math.e)

A_HEADS = 16
A_HEAD_DIM = 128
A_WIDTH = A_HEADS * A_HEAD_DIM
A_WINDOW = 8 * CHUNK
A_BAND = A_WINDOW + CHUNK
REL_CLIP = 128
A_SCALE = A_HEAD_DIM ** -0.5
A_PROJ_SEG = 256

B_HEADS = 16
B_NOPE = 128
B_ROPE = 64
B_V = 128
B_QK = B_NOPE + B_ROPE
B_Q_RANK = 512
B_KV_RANK = 512
B_WIDTH = B_HEADS * B_V
B_SCALE = B_QK ** -0.5
ROPE_THETA = 10000.0
B_CAT = B_NOPE + 2 * B_ROPE
BF16_SUBLANES = 16
VT_ROWS = B_V + BF16_SUBLANES

LANES = 128
VMEM_LIMIT = 56 * 1024 * 1024

F32 = jnp.float32
BF16 = jnp.bfloat16
_NT = (((1,), (1,)), ((), ()))


def _rms_scale(x):
    return lax.rsqrt(jnp.mean(x * x, axis=-1, keepdims=True) + EPS)


def _const_spec(shape):
    return pl.BlockSpec(shape, lambda *_: (0,) * len(shape), pipeline_mode=pl.Buffered(1))


def _proj_a_kernel(x_ref, ln_ref, w_ref, gq_ref, gk_ref, q_ref, k_ref, v_ref, sg_ref, k32_ref, v32_ref, h_ref,
                   *, ts, tiles_per_keep, keep_rows):
    i = pl.program_id(0)
    j = pl.program_id(1)
    tm = x_ref.shape[0]

    @pl.when(j == 0)
    def _():
        x = x_ref[...]
        h_ref[...] = (x * _rms_scale(x) * ln_ref[...]).astype(BF16)

    h = h_ref[...]
    qk = jnp.dot(h, w_ref[:, :2 * ts], preferred_element_type=F32)
    gq = gq_ref[...] * (A_SCALE * LOG2E)
    gk = gk_ref[...]
    k_tails = []
    for hh in range(ts // LANES):
        sl = slice(hh * LANES, (hh + 1) * LANES)
        s = qk[:, sl]
        q_ref[:, sl] = (s * _rms_scale(s) * gq).astype(BF16)
        s = qk[:, ts + hh * LANES:ts + (hh + 1) * LANES]
        kn = s * _rms_scale(s) * gk
        k_ref[:, sl] = kn.astype(BF16)
        k_tails.append(kn[tm - keep_rows:, :])

    vg = jnp.dot(h, w_ref[:, 2 * ts:], preferred_element_type=F32)
    v = vg[:, :ts]
    g = vg[:, ts:]
    v_ref[...] = v.astype(BF16)
    sg_ref[...] = (g / (1.0 + jnp.exp(-g))).astype(BF16)

    @pl.when((i % tiles_per_keep) == (tiles_per_keep - 1))
    def _():
        for hh, kn_tail in enumerate(k_tails):
            k32_ref[:, hh * LANES:(hh + 1) * LANES] = kn_tail
        v32_ref[...] = v[tm - keep_rows:, :]


def _proj_a(x, ln, w, gq, gk, *, tm, ts, keep_rows, tiles_per_keep):
    t = x.shape[0]
    nm = t // tm
    nkeep = nm // tiles_per_keep

    def tail_map(i, j):
        is_tail = (i % tiles_per_keep) == (tiles_per_keep - 1)
        return (i // tiles_per_keep, jnp.where(is_tail, j, 0))

    kernel = functools.partial(_proj_a_kernel, ts=ts, tiles_per_keep=tiles_per_keep, keep_rows=keep_rows)
    act = jax.ShapeDtypeStruct((t, A_WIDTH), BF16)
    tail = jax.ShapeDtypeStruct((nkeep * keep_rows, A_WIDTH), F32)
    return pl.pallas_call(
        kernel,
        grid=(nm, A_WIDTH // ts),
        in_specs=[
            pl.BlockSpec((tm, D_MODEL), lambda i, j: (i, 0)),
            _const_spec((1, D_MODEL)),
            pl.BlockSpec((D_MODEL, 4 * ts), lambda i, j: (0, j)),
            _const_spec((1, A_HEAD_DIM)),
            _const_spec((1, A_HEAD_DIM)),
        ],
        out_specs=[pl.BlockSpec((tm, ts), lambda i, j: (i, j))] * 4
                  + [pl.BlockSpec((keep_rows, ts), tail_map)] * 2,
        out_shape=[act] * 4 + [tail] * 2,
        scratch_shapes=[pltpu.VMEM((tm, D_MODEL), BF16)],
        compiler_params=pltpu.CompilerParams(
            dimension_semantics=("arbitrary", "arbitrary"), vmem_limit_bytes=VMEM_LIMIT),
        name="proj_a",
    )(x, ln, w, gq, gk)


def _attn_a_kernel(*refs, has_cache, tq, seq, n_bias, heads):
    if has_cache:
        q_ref, k_ref, v_ref, kc_ref, vc_ref, bias_ref, o_ref, kpad, vpad, s_ref = refs
    else:
        q_ref, k_ref, v_ref, bias_ref, o_ref, kpad, vpad, s_ref = refs
    width = A_WINDOW + tq
    n_blocks = seq // tq

    for hh in range(heads):
        cols = slice(hh * A_HEAD_DIM, (hh + 1) * A_HEAD_DIM)
        if has_cache:
            kpad[0:A_WINDOW, :] = kc_ref[0, :, hh, :].astype(BF16)
            vpad[0:A_WINDOW, :] = vc_ref[0, :, hh, :].astype(BF16)
        else:
            kpad[0:A_WINDOW, :] = jnp.zeros((A_WINDOW, A_HEAD_DIM), BF16)
            vpad[0:A_WINDOW, :] = jnp.zeros((A_WINDOW, A_HEAD_DIM), BF16)
        kpad[A_WINDOW:A_WINDOW + seq, :] = k_ref[0, :, cols]
        vpad[A_WINDOW:A_WINDOW + seq, :] = v_ref[0, :, cols]

        def scores(t):
            q = q_ref[0, t * tq:(t + 1) * tq, cols]
            kb = kpad[t * tq:t * tq + width, :]
            s_ref[t % 2] = lax.dot_general(q, kb, _NT, preferred_element_type=F32)

        scores(0)
        for t in range(n_blocks):
            if t + 1 < n_blocks:
                scores(t + 1)
            s = s_ref[t % 2] + bias_ref[hh, min(t, n_bias - 1)]
            m = jnp.max(s, axis=-1, keepdims=True)
            p = jnp.exp2(s - m)
            l = jnp.sum(p, axis=-1, keepdims=True)
            vb = vpad[t * tq:t * tq + width, :]
            o = jnp.dot(p.astype(BF16), vb, preferred_element_type=F32) / l
            o_ref[0, t * tq:(t + 1) * tq, cols] = o.astype(BF16)


def _attn_a(q, k, v, bias, k_cache, v_cache, *, batch, seq, tq, heads):
    has_cache = k_cache is not None
    width = A_WINDOW + tq
    n_bias = bias.shape[1]
    in_specs = [pl.BlockSpec((1, seq, heads * A_HEAD_DIM), lambda b, h: (b, 0, h))] * 3
    args = [a.reshape(batch, seq, A_WIDTH) for a in (q, k, v)]
    if has_cache:
        in_specs += [pl.BlockSpec((1, A_WINDOW, heads, A_HEAD_DIM), lambda b, h: (b, 0, h, 0))] * 2
        args += [k_cache, v_cache]
    in_specs.append(pl.BlockSpec((heads, n_bias, tq, width), lambda b, h: (h, 0, 0, 0)))
    args.append(bias)
    kernel = functools.partial(_attn_a_kernel, has_cache=has_cache, tq=tq, seq=seq, n_bias=n_bias, heads=heads)
    out = pl.pallas_call(
        kernel,
        grid=(batch, A_HEADS // heads),
        in_specs=in_specs,
        out_specs=pl.BlockSpec((1, seq, heads * A_HEAD_DIM), lambda b, h: (b, 0, h)),
        out_shape=jax.ShapeDtypeStruct((batch, seq, A_WIDTH), BF16),
        scratch_shapes=[pltpu.VMEM((A_WINDOW + seq, A_HEAD_DIM), BF16),
                        pltpu.VMEM((A_WINDOW + seq, A_HEAD_DIM), BF16),
                        pltpu.VMEM((2, tq, width), F32)],
        compiler_params=pltpu.CompilerParams(
            dimension_semantics=("arbitrary", "arbitrary"), vmem_limit_bytes=VMEM_LIMIT),
        name="attn_a",
    )(*args)
    return out.reshape(batch * seq, A_WIDTH)


def _band_bias(table, tq, mask_prefix):
    width = A_WINDOW + tq
    length = tq + width - 1
    n_table = 2 * REL_CLIP + 1
    c0 = A_WINDOW + REL_CLIP - (width - 1)
    n_lo, first = max(0, -c0), max(0, c0)
    n_hi = length - n_lo - (n_table - first)
    heads = table.shape[0]
    v = jnp.concatenate([jnp.broadcast_to(table[:, :1], (heads, n_lo)), table[:, first:],
                         jnp.broadcast_to(table[:, -1:], (heads, n_hi))], axis=1).astype(F32)
    w = jnp.pad(v[:, ::-1], ((0, 0), (0, 1)))
    skew = jnp.tile(w, (1, tq))[:, :tq * length].reshape(heads, tq, length)
    toeplitz = skew[:, :, tq - 1:tq - 1 + width] * LOG2E

    r = np.arange(tq)[:, None]
    j = np.arange(width)[None, :]
    in_band = (j >= CHUNK * (r // CHUNK)) & (j < CHUNK * (r // CHUNK) + A_BAND)
    n = A_WINDOW // tq + 1 if mask_prefix else 1
    keep = np.stack([in_band & (j + v_idx * tq >= A_WINDOW if mask_prefix else True) for v_idx in range(n)])
    return jnp.where(keep[None], toeplitz[:, None], NEG_INF)


def _out_proj_kernel(x_ref, sg_ref, o_ref, w_ref, y_ref):
    gated = sg_ref[...] * o_ref[...]
    y_ref[...] = x_ref[...] + jnp.dot(gated, w_ref[...], preferred_element_type=F32)


def _out_proj(x, sg, o, w, *, tm):
    t, width = o.shape
    return pl.pallas_call(
        _out_proj_kernel,
        grid=(t // tm,),
        in_specs=[
            pl.BlockSpec((tm, D_MODEL), lambda i: (i, 0)),
            pl.BlockSpec((tm, width), lambda i: (i, 0)),
            pl.BlockSpec((tm, width), lambda i: (i, 0)),
            _const_spec((width, D_MODEL)),
        ],
        out_specs=pl.BlockSpec((tm, D_MODEL), lambda i: (i, 0)),
        out_shape=jax.ShapeDtypeStruct((t, D_MODEL), F32),
        compiler_params=pltpu.CompilerParams(
            dimension_semantics=("arbitrary",), vmem_limit_bytes=VMEM_LIMIT),
        name="out_proj",
    )(x, sg, o, w)


Q_HEADS_PER_DOT = 4
GATE_CHUNK = 512


def _proj_b_kernel(x_ref, ln_ref, w1_ref, wg_ref, wuq_ref, gqa_ref, gkva_ref, gq_ref, gkr_ref, cs_ref, csq_ref,
                   q_ref, ckv_ref, kr2_ref, sg_ref, *, transposed_q):
    x = x_ref[...]
    h = (x * _rms_scale(x) * ln_ref[...]).astype(BF16)
    t1 = jnp.dot(h, w1_ref[...], preferred_element_type=F32)
    cq = t1[:, :B_Q_RANK]
    ckv = t1[:, B_Q_RANK:B_Q_RANK + B_KV_RANK]
    kr = t1[:, B_Q_RANK + B_KV_RANK:]

    ckv_ref[...] = ckv * _rms_scale(ckv) * gkva_ref[...]
    wk = kr * _rms_scale(kr) * gkr_ref[...] * cs_ref[...]
    kr2_ref[...] = wk + pltpu.roll(wk, B_ROPE, axis=1)

    cqn = (cq * _rms_scale(cq) * gqa_ref[...]).astype(BF16)
    if transposed_q:
        g_nope = gq_ref[:B_NOPE, :] * (B_SCALE * LOG2E)
        g_rope = gq_ref[B_NOPE:, :] * csq_ref[...] * (B_SCALE * LOG2E)
        rows = Q_HEADS_PER_DOT * B_CAT
        for grp in range(B_HEADS // Q_HEADS_PER_DOT):
            qt = lax.dot_general(wuq_ref[grp * rows:(grp + 1) * rows, :], cqn, _NT, preferred_element_type=F32)
            for hh in range(Q_HEADS_PER_DOT):
                head = grp * Q_HEADS_PER_DOT + hh
                qn = qt[hh * B_CAT:hh * B_CAT + B_NOPE, :]
                qr = qt[hh * B_CAT + B_NOPE:(hh + 1) * B_CAT, :]
                sn = lax.rsqrt(jnp.mean(qn * qn, axis=0, keepdims=True) + EPS)
                sr = lax.rsqrt(jnp.mean(qr * qr, axis=0, keepdims=True) + EPS)
                q_ref[0, head, 0:B_NOPE, :] = (qn * sn * g_nope).astype(BF16)
                q_ref[0, head, B_NOPE:B_CAT, :] = (qr * sr * g_rope).astype(BF16)
    else:
        q = jnp.dot(cqn, wuq_ref[...], preferred_element_type=F32)
        g_nope = gq_ref[:, :B_NOPE] * (B_SCALE * LOG2E)
        g_rope = gq_ref[:, B_NOPE:] * csq_ref[...] * (B_SCALE * LOG2E)
        for hh in range(B_HEADS):
            qn = q[:, hh * B_CAT:hh * B_CAT + B_NOPE]
            qr = q[:, hh * B_CAT + B_NOPE:(hh + 1) * B_CAT]
            q_ref[:, hh * B_CAT:hh * B_CAT + B_NOPE] = (qn * _rms_scale(qn) * g_nope).astype(BF16)
            q_ref[:, hh * B_CAT + B_NOPE:(hh + 1) * B_CAT] = (qr * _rms_scale(qr) * g_rope).astype(BF16)

    for c in range(B_WIDTH // GATE_CHUNK):
        sl = slice(c * GATE_CHUNK, (c + 1) * GATE_CHUNK)
        g = jnp.dot(h, wg_ref[:, sl], preferred_element_type=F32)
        sg_ref[:, sl] = (g / (1.0 + jnp.exp(-g))).astype(BF16)


def _proj_b(x, ln, w1, wg, wuq, gqa, gkva, gq, gkr, cs, *, tm, batch=None):
    t = x.shape[0]
    n_pos_tiles = cs.shape[0] // tm
    transposed_q = batch is not None
    pos_spec = pl.BlockSpec((tm, 2 * B_ROPE), lambda i: (i % n_pos_tiles, 0))
    if transposed_q:
        tiles = t // batch // tm
        wuq = wuq.T
        gq = jnp.broadcast_to(gq.reshape(B_CAT, 1), (B_CAT, tm))
        csq = cs.T
        gq_spec = _const_spec((B_CAT, tm))
        csq_spec = pl.BlockSpec((2 * B_ROPE, tm), lambda i: (0, i % n_pos_tiles))
        q_spec = pl.BlockSpec((1, B_HEADS, B_CAT, tm), lambda i: (i // tiles, 0, 0, i % tiles))
        q_shape = jax.ShapeDtypeStruct((batch, B_HEADS, B_CAT, t // batch), BF16)
    else:
        csq = cs
        gq_spec = _const_spec((1, B_CAT))
        csq_spec = pos_spec
        q_spec = pl.BlockSpec((tm, B_HEADS * B_CAT), lambda i: (i, 0))
        q_shape = jax.ShapeDtypeStruct((t, B_HEADS * B_CAT), BF16)
    return pl.pallas_call(
        functools.partial(_proj_b_kernel, transposed_q=transposed_q),
        grid=(t // tm,),
        in_specs=[
            pl.BlockSpec((tm, D_MODEL), lambda i: (i, 0)),
            _const_spec((1, D_MODEL)),
            _const_spec(w1.shape),
            _const_spec(wg.shape),
            _const_spec(wuq.shape),
            _const_spec((1, B_Q_RANK)),
            _const_spec((1, B_KV_RANK)),
            gq_spec,
            _const_spec((1, 2 * B_ROPE)),
            pos_spec,
            csq_spec,
        ],
        out_specs=[
            q_spec,
            pl.BlockSpec((tm, B_KV_RANK), lambda i: (i, 0)),
            pl.BlockSpec((tm, 2 * B_ROPE), lambda i: (i, 0)),
            pl.BlockSpec((tm, B_WIDTH), lambda i: (i, 0)),
        ],
        out_shape=[
            q_shape,
            jax.ShapeDtypeStruct((t, B_KV_RANK), F32),
            jax.ShapeDtypeStruct((t, 2 * B_ROPE), F32),
            jax.ShapeDtypeStruct((t, B_WIDTH), BF16),
        ],
        compiler_params=pltpu.CompilerParams(
            dimension_semantics=("arbitrary",), vmem_limit_bytes=VMEM_LIMIT),
        name="proj_b",
    )(x, ln, w1, wg, wuq, gqa, gkva, gq, gkr, cs, csq)


def _expand_b_kernel(ckv_ref, kr2_ref, wuk_ref, wuv_ref, gkn_ref, kcat_ref, v_ref, *, transposed_v):
    c = ckv_ref[...].astype(BF16)
    kn = jnp.dot(c, wuk_ref[...], preferred_element_type=F32)
    krb = kr2_ref[...].astype(BF16)
    g = gkn_ref[...]
    for hh in range(B_HEADS):
        s = kn[:, hh * B_NOPE:(hh + 1) * B_NOPE]
        kcat_ref[:, hh * B_CAT:hh * B_CAT + B_NOPE] = (s * _rms_scale(s) * g).astype(BF16)
        kcat_ref[:, hh * B_CAT + B_NOPE:(hh + 1) * B_CAT] = krb
    if transposed_v:
        vt = lax.dot_general(wuv_ref[...], c, _NT, preferred_element_type=F32).astype(BF16)
        tm = c.shape[0]
        for hh in range(B_HEADS):
            v_ref[0, hh, 0, 0:B_V, :] = vt[hh * B_V:(hh + 1) * B_V, :]
            v_ref[0, hh, 0, B_V:VT_ROWS, :] = jnp.ones((VT_ROWS - B_V, tm), BF16)
    else:
        v_ref[...] = jnp.dot(c, wuv_ref[...], preferred_element_type=F32).astype(BF16)


def _expand_b(ckv, kr2, wuk, wuv, gkn, *, tm, batch=None):
    t = ckv.shape[0]
    transposed_v = batch is not None
    if transposed_v:
        tiles = t // batch // tm
        v_spec = pl.BlockSpec((1, B_HEADS, 1, VT_ROWS, tm), lambda i: (i // tiles, 0, i % tiles, 0, 0))
        v_shape = jax.ShapeDtypeStruct((batch, B_HEADS, tiles, VT_ROWS, tm), BF16)
    else:
        v_spec = pl.BlockSpec((tm, B_WIDTH), lambda i: (i, 0))
        v_shape = jax.ShapeDtypeStruct((t, B_WIDTH), BF16)
    return pl.pallas_call(
        functools.partial(_expand_b_kernel, transposed_v=transposed_v),
        grid=(t // tm,),
        in_specs=[
            pl.BlockSpec((tm, B_KV_RANK), lambda i: (i, 0)),
            pl.BlockSpec((tm, 2 * B_ROPE), lambda i: (i, 0)),
            _const_spec(wuk.shape),
            _const_spec(wuv.shape),
            _const_spec((1, B_NOPE)),
        ],
        out_specs=[pl.BlockSpec((tm, B_HEADS * B_CAT), lambda i: (i, 0)), v_spec],
        out_shape=[jax.ShapeDtypeStruct((t, B_HEADS * B_CAT), BF16), v_shape],
        compiler_params=pltpu.CompilerParams(
            dimension_semantics=("arbitrary",), vmem_limit_bytes=VMEM_LIMIT),
        name="expand_b",
    )(ckv, kr2, wuk, wuv, gkn)


def _attn_bt_kernel(q_ref, k_ref, vt_ref, o_ref, s_ref, mask_ref, m_ref, acc_ref, *, tq, n_blocks):
    pairs = [(qi, kj) for qi in range(n_blocks) for kj in range(qi + 1)]

    key_chunk = lax.broadcasted_iota(jnp.int32, (tq, tq), 0) // CHUNK
    q_chunk = lax.broadcasted_iota(jnp.int32, (tq, tq), 1) // CHUNK
    mask_ref[...] = jnp.where(key_chunk <= q_chunk, 0.0, NEG_INF).astype(F32)

    def scores(t):
        qi, kj = pairs[t]
        kb = k_ref[0, kj * tq:(kj + 1) * tq, :]
        q_t = q_ref[0, 0, :, qi * tq:(qi + 1) * tq]
        s_ref[t % 2] = jnp.dot(kb, q_t, preferred_element_type=F32)

    scores(0)
    for t, (qi, kj) in enumerate(pairs):
        if t + 1 < len(pairs):
            scores(t + 1)
        s = s_ref[t % 2]
        if kj == qi:
            s = s + mask_ref[...]
        col_max = jnp.max(s, axis=0, keepdims=True)
        m_new = col_max if kj == 0 else jnp.maximum(m_ref[...], col_max)
        p = jnp.exp2(s - m_new).astype(BF16)
        pv = jnp.dot(vt_ref[0, 0, kj], p, preferred_element_type=F32)
        acc = pv if kj == 0 else jnp.exp2(m_ref[...] - m_new) * acc_ref[...] + pv
        if kj == qi:
            o_t = acc[:B_V, :] / acc[B_V:B_V + 1, :]
            o_ref[0, qi * tq:(qi + 1) * tq, :] = o_t.T.astype(BF16)
        else:
            acc_ref[...] = acc
            m_ref[...] = m_new


def _attn_bt(qt, kcat, vt, *, batch, seq, tq):
    k3 = kcat.reshape(batch, seq, B_HEADS * B_CAT)
    n_blocks = seq // tq
    out = pl.pallas_call(
        functools.partial(_attn_bt_kernel, tq=tq, n_blocks=n_blocks),
        grid=(batch, B_HEADS),
        in_specs=[
            pl.BlockSpec((1, 1, B_CAT, seq), lambda b, h: (b, h, 0, 0)),
            pl.BlockSpec((1, seq, B_CAT), lambda b, h: (b, 0, h)),
            pl.BlockSpec((1, 1, n_blocks, VT_ROWS, tq), lambda b, h: (b, h, 0, 0, 0)),
        ],
        out_specs=pl.BlockSpec((1, seq, B_V), lambda b, h: (b, 0, h)),
        out_shape=jax.ShapeDtypeStruct((batch, seq, B_WIDTH), BF16),
        scratch_shapes=[pltpu.VMEM((2, tq, tq), F32), pltpu.VMEM((tq, tq), F32),
                        pltpu.VMEM((1, tq), F32), pltpu.VMEM((VT_ROWS, tq), F32)],
        compiler_params=pltpu.CompilerParams(
            dimension_semantics=("arbitrary", "arbitrary"), vmem_limit_bytes=VMEM_LIMIT),
        name="attn_bt",
    )(qt, k3, vt)
    return out.reshape(batch * seq, B_WIDTH)


def _attn_b_kernel(q_ref, k_ref, v_ref, o_ref):
    for hh in range(B_HEADS):
        q = q_ref[0, :, hh * B_CAT:(hh + 1) * B_CAT]
        kb = k_ref[0, :, hh * B_CAT:(hh + 1) * B_CAT]
        s = lax.dot_general(q, kb, _NT, preferred_element_type=F32)
        p = jnp.exp2(s - jnp.max(s, axis=-1, keepdims=True))
        l = jnp.sum(p, axis=-1, keepdims=True)
        vb = v_ref[0, :, hh * B_V:(hh + 1) * B_V]
        o = jnp.dot(p.astype(BF16), vb, preferred_element_type=F32) / l
        o_ref[0, :, hh * B_V:(hh + 1) * B_V] = o.astype(BF16)


def _attn_b(qcat, kcat, v, *, batch, q_len, k_len):
    q3 = qcat.reshape(batch, q_len, B_HEADS * B_CAT)
    k3 = kcat.reshape(batch, k_len, B_HEADS * B_CAT)
    v3 = v.reshape(batch, k_len, B_WIDTH)
    out = pl.pallas_call(
        _attn_b_kernel,
        grid=(batch,),
        in_specs=[
            pl.BlockSpec((1, q_len, B_HEADS * B_CAT), lambda b: (b, 0, 0)),
            pl.BlockSpec((1, k_len, B_HEADS * B_CAT), lambda b: (b, 0, 0)),
            pl.BlockSpec((1, k_len, B_WIDTH), lambda b: (b, 0, 0)),
        ],
        out_specs=pl.BlockSpec((1, q_len, B_WIDTH), lambda b: (b, 0, 0)),
        out_shape=jax.ShapeDtypeStruct((batch, q_len, B_WIDTH), BF16),
        compiler_params=pltpu.CompilerParams(
            dimension_semantics=("arbitrary",), vmem_limit_bytes=VMEM_LIMIT),
        name="attn_b",
    )(q3, k3, v3)
    return out.reshape(batch * q_len, B_WIDTH)


def _swap_halves(a):
    half = a.shape[-1] // 2
    return jnp.concatenate([a[..., half:], a[..., :half]], axis=-1)


def _rope_table(pos):
    half = B_ROPE // 2
    inv = ROPE_THETA ** (-jnp.arange(half, dtype=F32) / half)
    ang = pos.astype(F32)[:, None] * inv[None, :]
    cos, sin = jnp.cos(ang), jnp.sin(ang)
    return jnp.concatenate([cos, cos, -sin, sin], axis=-1)


def _row(v):
    return v.reshape(1, -1).astype(F32)


def _layer_a(x, k_cache, v_cache, ln, w_in, gq, gk, table, w_out, *, batch, seq, tm, tq, tm_out):
    if k_cache is None:
        keep = min(A_WINDOW, seq)
        keep_rows, tiles_per_keep = keep, seq // tm
    else:
        keep = seq
        keep_rows, tiles_per_keep = tm, 1
    q, k, v, sg, k32, v32 = _proj_a(x, ln, w_in, gq, gk, tm=tm, ts=A_PROJ_SEG,
                                    keep_rows=keep_rows, tiles_per_keep=tiles_per_keep)
    bias = _band_bias(table, tq, mask_prefix=k_cache is None)
    o = _attn_a(q, k, v, bias, k_cache, v_cache, batch=batch, seq=seq, tq=tq,
                heads=1 if k_cache is None else A_HEADS)
    y = _out_proj(x, sg, o, w_out, tm=tm_out)
    shape = (batch, keep, A_HEADS, A_HEAD_DIM)
    return y, k32.reshape(shape), v32.reshape(shape)


def _layer_b(x, ckv_cache, kr_cache, pos, ln, w1, wg, wuq, gqa, gkva, gq, gkr, wuk, wuv, gkn, w_out,
             *, batch, seq, tm, tm_out):
    cs = _rope_table(pos)
    if ckv_cache is not None:
        cs = jnp.tile(cs, (batch, 1))
    qcat, ckv, kr2, sg = _proj_b(x, ln, w1, wg, wuq, gqa, gkva, gq, gkr, cs, tm=tm,
                                 batch=batch if ckv_cache is None else None)
    if ckv_cache is None:
        tq = 512
        kcat, vt = _expand_b(ckv, kr2, wuk, wuv.T, gkn, tm=tq, batch=batch)
        o = _attn_bt(qcat, kcat, vt, batch=batch, seq=seq, tq=tq)
    else:
        k_len = ckv_cache.shape[1] + seq
        ckv_all = jnp.concatenate(
            [ckv_cache, ckv.reshape(batch, seq, B_KV_RANK)], axis=1).reshape(batch * k_len, B_KV_RANK)
        kr2_all = jnp.concatenate(
            [jnp.concatenate([kr_cache, kr_cache], axis=-1), kr2.reshape(batch, seq, 2 * B_ROPE)],
            axis=1).reshape(batch * k_len, 2 * B_ROPE)
        kcat, v = _expand_b(ckv_all, kr2_all, wuk, wuv, gkn, tm=512)
        o = _attn_b(qcat, kcat, v, batch=batch, q_len=seq, k_len=k_len)
    y = _out_proj(x, sg, o, w_out, tm=tm_out)
    return y, ckv.reshape(batch, seq, B_KV_RANK), kr2[:, :B_ROPE].reshape(batch, seq, B_ROPE)


def kernel(x_prompt, x_sample, cache_a_k, cache_a_v, cache_b_ckv, cache_b_krope, a_ln, w_a_in, a_q_norm, a_k_norm, a_rel_bias, w_a_out, b_ln, w_b_in, b_q_a_norm, w_b_uq, b_kv_a_norm, w_b_uk, w_b_uv, b_q_nope_norm, b_k_nope_norm, b_q_rope_norm, b_k_rope_norm, w_b_out):
    batch, seq, _ = x_prompt.shape
    dec_batch, dec_seq, _ = x_sample.shape
    dec_tokens = dec_batch * dec_seq
    depth = a_ln.shape[0] + b_ln.shape[0]
    yp = x_prompt.reshape(batch * seq, D_MODEL)
    ys = x_sample.reshape(dec_tokens, D_MODEL)
    akp, avp, aks, avs, bcp, brp, bcs, brs = ([] for _ in range(8))
    for layer in range(depth):
        i = layer // 2
        if layer % 2 == 0:
            w_in = w_a_in[i].reshape(D_MODEL, 4, A_WIDTH // A_PROJ_SEG, A_PROJ_SEG)
            w_in = w_in.transpose(0, 2, 1, 3).reshape(D_MODEL, 4 * A_WIDTH).astype(BF16)
            params = (_row(a_ln[i]), w_in, _row(a_q_norm[i]), _row(a_k_norm[i]),
                      a_rel_bias[i], w_a_out[i].astype(BF16))
            yp, k_p, v_p = _layer_a(yp, None, None, *params, batch=batch, seq=seq,
                                    tm=1024, tq=4 * CHUNK, tm_out=512)
            ys, k_s, v_s = _layer_a(ys, cache_a_k[i], cache_a_v[i], *params, batch=dec_batch, seq=dec_seq,
                                    tm=dec_tokens, tq=CHUNK, tm_out=dec_tokens)
            akp.append(k_p); avp.append(v_p); aks.append(k_s); avs.append(v_s)
        else:
            w_in = w_b_in[i]
            c1, c2 = B_Q_RANK + B_KV_RANK, B_Q_RANK + B_KV_RANK + B_ROPE
            w_kr = w_in[:, c1:c2]
            w1 = jnp.concatenate([w_in[:, :c1], w_kr, _swap_halves(w_kr)], axis=1).astype(BF16)
            wg = w_in[:, c2:].astype(BF16)
            wuq = w_b_uq[i].reshape(B_Q_RANK, B_HEADS, B_QK)
            wuq_rope = wuq[..., B_NOPE:]
            wuq = jnp.concatenate([wuq[..., :B_NOPE], wuq_rope, _swap_halves(wuq_rope)],
                                  axis=-1).reshape(B_Q_RANK, B_HEADS * B_CAT).astype(BF16)
            g_qr, g_kr = b_q_rope_norm[i], b_k_rope_norm[i]
            gq = _row(jnp.concatenate([b_q_nope_norm[i], g_qr, _swap_halves(g_qr)]))
            gkr = _row(jnp.concatenate([g_kr, _swap_halves(g_kr)]))
            params = (_row(b_ln[i]), w1, wg, wuq, _row(b_q_a_norm[i]), _row(b_kv_a_norm[i]), gq, gkr,
                      w_b_uk[i].astype(BF16), w_b_uv[i].astype(BF16), _row(b_k_nope_norm[i]),
                      w_b_out[i].astype(BF16))
            yp, c_p, r_p = _layer_b(yp, None, None, jnp.arange(seq), *params, batch=batch, seq=seq,
                                    tm=512, tm_out=512)
            ys, c_s, r_s = _layer_b(ys, cache_b_ckv[i], cache_b_krope[i], PAST_LEN + jnp.arange(dec_seq),
                                    *params, batch=dec_batch, seq=dec_seq, tm=256, tm_out=dec_tokens)
            bcp.append(c_p); brp.append(r_p); bcs.append(c_s); brs.append(r_s)
    yp = yp.reshape(batch, seq, D_MODEL)
    ys = ys.reshape(dec_batch, dec_seq, D_MODEL)
    return (yp, ys,
            jnp.stack(akp), jnp.stack(avp), jnp.stack(bcp), jnp.stack(brp),
            jnp.stack(aks), jnp.stack(avs), jnp.stack(bcs), jnp.stack(brs))
```

```python
import functools
import math

import numpy as np

import jax
import jax.numpy as jnp
from jax import lax
from jax.experimental import pallas as pl
from jax.experimental.pallas import tpu as pltpu

D_MODEL = 2048
CHUNK = 64
EPS = 1e-6
NEG_INF = -1e30
PAST_LEN = 1024
LOG2E = math.log2(math.e)

A_HEADS = 16
A_HEAD_DIM = 128
A_WIDTH = A_HEADS * A_HEAD_DIM
A_WINDOW = 8 * CHUNK
A_BAND = A_WINDOW + CHUNK
REL_CLIP = 128
A_SCALE = A_HEAD_DIM ** -0.5
A_PROJ_SEG = 256

B_HEADS = 16
B_NOPE = 128
B_ROPE = 64
B_V = 128
B_QK = B_NOPE + B_ROPE
B_Q_RANK = 512
B_KV_RANK = 512
B_WIDTH = B_HEADS * B_V
B_SCALE = B_QK ** -0.5
ROPE_THETA = 10000.0
B_CAT = B_NOPE + 2 * B_ROPE
BF16_SUBLANES = 16
VT_ROWS = B_V + BF16_SUBLANES

LANES = 128
VMEM_LIMIT = 56 * 1024 * 1024

F32 = jnp.float32
BF16 = jnp.bfloat16
_NT = (((1,), (1,)), ((), ()))


def _rms_scale(x):
    return lax.rsqrt(jnp.mean(x * x, axis=-1, keepdims=True) + EPS)


def _const_spec(shape):
    return pl.BlockSpec(shape, lambda *_: (0,) * len(shape), pipeline_mode=pl.Buffered(1))


def _proj_a_kernel(x_ref, ln_ref, wq_ref, wk_ref, wv_ref, wg_ref, gq_ref, gk_ref,
                   q_ref, k_ref, v_ref, sg_ref, k32_ref, v32_ref, h_ref, *, ts, tiles_per_keep, keep_rows):
    i = pl.program_id(0)
    j = pl.program_id(1)
    tm = x_ref.shape[0]

    @pl.when(j == 0)
    def _():
        x = x_ref[...]
        h_ref[...] = (x * _rms_scale(x) * ln_ref[...]).astype(BF16)

    h = h_ref[...]
    q = jnp.dot(h, wq_ref[...], preferred_element_type=F32)
    gq = gq_ref[...] * (A_SCALE * LOG2E)
    for hh in range(ts // LANES):
        sl = slice(hh * LANES, (hh + 1) * LANES)
        s = q[:, sl]
        q_ref[:, sl] = (s * _rms_scale(s) * gq).astype(BF16)

    k = jnp.dot(h, wk_ref[...], preferred_element_type=F32)
    gk = gk_ref[...]
    k_tails = []
    for hh in range(ts // LANES):
        sl = slice(hh * LANES, (hh + 1) * LANES)
        s = k[:, sl]
        kn = s * _rms_scale(s) * gk
        k_ref[:, sl] = kn.astype(BF16)
        k_tails.append(kn[tm - keep_rows:, :])

    v = jnp.dot(h, wv_ref[...], preferred_element_type=F32)
    v_ref[...] = v.astype(BF16)
    g = jnp.dot(h, wg_ref[...], preferred_element_type=F32)
    sg_ref[...] = (g / (1.0 + jnp.exp(-g))).astype(BF16)

    @pl.when((i % tiles_per_keep) == (tiles_per_keep - 1))
    def _():
        for hh, kn_tail in enumerate(k_tails):
            k32_ref[:, hh * LANES:(hh + 1) * LANES] = kn_tail
        v32_ref[...] = v[tm - keep_rows:, :]


def _proj_a(x, ln, w, gq, gk, *, tm, ts, keep_rows, tiles_per_keep):
    t = x.shape[0]
    nm = t // tm
    nkeep = nm // tiles_per_keep
    nj = A_WIDTH // ts

    def w_spec(part):
        return pl.BlockSpec((D_MODEL, ts), lambda i, j: (0, part * nj + j))

    def tail_map(i, j):
        is_tail = (i % tiles_per_keep) == (tiles_per_keep - 1)
        return (i // tiles_per_keep, jnp.where(is_tail, j, 0))

    kernel = functools.partial(_proj_a_kernel, ts=ts, tiles_per_keep=tiles_per_keep, keep_rows=keep_rows)
    act = jax.ShapeDtypeStruct((t, A_WIDTH), BF16)
    tail = jax.ShapeDtypeStruct((nkeep * keep_rows, A_WIDTH), F32)
    return pl.pallas_call(
        kernel,
        grid=(nm, nj),
        in_specs=[
            pl.BlockSpec((tm, D_MODEL), lambda i, j: (i, 0)),
            _const_spec((1, D_MODEL)),
            w_spec(0), w_spec(1), w_spec(2), w_spec(3),
            _const_spec((1, A_HEAD_DIM)),
            _const_spec((1, A_HEAD_DIM)),
        ],
        out_specs=[pl.BlockSpec((tm, ts), lambda i, j: (i, j))] * 4
                  + [pl.BlockSpec((keep_rows, ts), tail_map)] * 2,
        out_shape=[act] * 4 + [tail] * 2,
        scratch_shapes=[pltpu.VMEM((tm, D_MODEL), BF16)],
        compiler_params=pltpu.CompilerParams(
            dimension_semantics=("arbitrary", "arbitrary"), vmem_limit_bytes=VMEM_LIMIT),
        name="proj_a",
    )(x, ln, w, w, w, w, gq, gk)


def _attn_a_kernel(*refs, has_cache, tq, seq, n_bias, heads):
    if has_cache:
        q_ref, k_ref, v_ref, kc_ref, vc_ref, bias_ref, o_ref, kpad, vpad, s_ref = refs
    else:
        q_ref, k_ref, v_ref, bias_ref, o_ref, kpad, vpad, s_ref = refs
    width = A_WINDOW + tq
    n_blocks = seq // tq

    for hh in range(heads):
        cols = slice(hh * A_HEAD_DIM, (hh + 1) * A_HEAD_DIM)
        if has_cache:
            kpad[0:A_WINDOW, :] = kc_ref[0, :, hh, :].astype(BF16)
            vpad[0:A_WINDOW, :] = vc_ref[0, :, hh, :].astype(BF16)
        else:
            kpad[0:A_WINDOW, :] = jnp.zeros((A_WINDOW, A_HEAD_DIM), BF16)
            vpad[0:A_WINDOW, :] = jnp.zeros((A_WINDOW, A_HEAD_DIM), BF16)
        kpad[A_WINDOW:A_WINDOW + seq, :] = k_ref[0, :, cols]
        vpad[A_WINDOW:A_WINDOW + seq, :] = v_ref[0, :, cols]

        def scores(t):
            q = q_ref[0, t * tq:(t + 1) * tq, cols]
            kb = kpad[t * tq:t * tq + width, :]
            s_ref[t % 2] = lax.dot_general(q, kb, _NT, preferred_element_type=F32)

        scores(0)
        for t in range(n_blocks):
            if t + 1 < n_blocks:
                scores(t + 1)
            s = s_ref[t % 2] + bias_ref[hh, min(t, n_bias - 1)]
            m = jnp.max(s, axis=-1, keepdims=True)
            p = jnp.exp2(s - m)
            l = jnp.sum(p, axis=-1, keepdims=True)
            vb = vpad[t * tq:t * tq + width, :]
            o = jnp.dot(p.astype(BF16), vb, preferred_element_type=F32) / l
            o_ref[0, t * tq:(t + 1) * tq, cols] = o.astype(BF16)


def _attn_a(q, k, v, bias, k_cache, v_cache, *, batch, seq, tq, heads):
    has_cache = k_cache is not None
    width = A_WINDOW + tq
    n_bias = bias.shape[1]
    in_specs = [pl.BlockSpec((1, seq, heads * A_HEAD_DIM), lambda b, h: (b, 0, h))] * 3
    args = [a.reshape(batch, seq, A_WIDTH) for a in (q, k, v)]
    if has_cache:
        in_specs += [pl.BlockSpec((1, A_WINDOW, heads, A_HEAD_DIM), lambda b, h: (b, 0, h, 0))] * 2
        args += [k_cache, v_cache]
    in_specs.append(pl.BlockSpec((heads, n_bias, tq, width), lambda b, h: (h, 0, 0, 0)))
    args.append(bias)
    kernel = functools.partial(_attn_a_kernel, has_cache=has_cache, tq=tq, seq=seq, n_bias=n_bias, heads=heads)
    out = pl.pallas_call(
        kernel,
        grid=(batch, A_HEADS // heads),
        in_specs=in_specs,
        out_specs=pl.BlockSpec((1, seq, heads * A_HEAD_DIM), lambda b, h: (b, 0, h)),
        out_shape=jax.ShapeDtypeStruct((batch, seq, A_WIDTH), BF16),
        scratch_shapes=[pltpu.VMEM((A_WINDOW + seq, A_HEAD_DIM), BF16),
                        pltpu.VMEM((A_WINDOW + seq, A_HEAD_DIM), BF16),
                        pltpu.VMEM((2, tq, width), F32)],
        compiler_params=pltpu.CompilerParams(
            dimension_semantics=("arbitrary", "arbitrary"), vmem_limit_bytes=VMEM_LIMIT),
        name="attn_a",
    )(*args)
    return out.reshape(batch * seq, A_WIDTH)


def _band_bias(table, tq, mask_prefix):
    width = A_WINDOW + tq
    length = CHUNK + A_BAND - 1
    n_table = 2 * REL_CLIP + 1
    c0 = A_WINDOW + REL_CLIP - (A_BAND - 1)
    n_lo, first = max(0, -c0), max(0, c0)
    n_hi = length - n_lo - (n_table - first)
    heads = table.shape[0]
    v = jnp.concatenate([jnp.broadcast_to(table[:, :1], (heads, n_lo)), table[:, first:],
                         jnp.broadcast_to(table[:, -1:], (heads, n_hi))], axis=1).astype(F32)
    w = jnp.pad(v[:, ::-1], ((0, 0), (0, 1)))
    skew = jnp.tile(w, (1, CHUNK))[:, :CHUNK * length].reshape(heads, CHUNK, length)
    band = skew[:, :, CHUNK - 1:CHUNK - 1 + A_BAND] * LOG2E
    chunks = [jnp.pad(band, ((0, 0), (0, 0), (c * CHUNK, tq - CHUNK - c * CHUNK)), constant_values=NEG_INF)
              for c in range(tq // CHUNK)]
    block = jnp.concatenate(chunks, axis=1)
    if not mask_prefix:
        return block[:, None]
    j = np.arange(width)[None, :]
    keep = np.stack([np.broadcast_to(j + v_idx * tq >= A_WINDOW, (tq, width))
                     for v_idx in range(A_WINDOW // tq + 1)])
    return jnp.where(keep[None], block[:, None], NEG_INF)


def _out_proj_kernel(x_ref, sg_ref, o_ref, w_ref, y_ref):
    gated = sg_ref[...] * o_ref[...]
    y_ref[...] = x_ref[...] + jnp.dot(gated, w_ref[...], preferred_element_type=F32)


def _out_proj(x, sg, o, w, *, tm):
    t, width = o.shape
    return pl.pallas_call(
        _out_proj_kernel,
        grid=(t // tm,),
        in_specs=[
            pl.BlockSpec((tm, D_MODEL), lambda i: (i, 0)),
            pl.BlockSpec((tm, width), lambda i: (i, 0)),
            pl.BlockSpec((tm, width), lambda i: (i, 0)),
            _const_spec((width, D_MODEL)),
        ],
        out_specs=pl.BlockSpec((tm, D_MODEL), lambda i: (i, 0)),
        out_shape=jax.ShapeDtypeStruct((t, D_MODEL), F32),
        compiler_params=pltpu.CompilerParams(
            dimension_semantics=("arbitrary",), vmem_limit_bytes=VMEM_LIMIT),
        name="out_proj",
    )(x, sg, o, w)


Q_HEADS_PER_DOT = 4
GATE_CHUNK = 512


def _proj_b_kernel(x_ref, ln_ref, w1_ref, wg_ref, wuq_ref, gqa_ref, gkva_ref, gq_ref, gkr_ref, cs_ref, csq_ref,
                   q_ref, ckv_ref, kr2_ref, sg_ref, *, transposed_q):
    x = x_ref[...]
    h = (x * _rms_scale(x) * ln_ref[...]).astype(BF16)
    t1 = jnp.dot(h, w1_ref[...], preferred_element_type=F32)
    cq = t1[:, :B_Q_RANK]
    ckv = t1[:, B_Q_RANK:B_Q_RANK + B_KV_RANK]
    kr = t1[:, B_Q_RANK + B_KV_RANK:]

    ckv_ref[...] = ckv * _rms_scale(ckv) * gkva_ref[...]
    wk = kr * _rms_scale(kr) * gkr_ref[...] * cs_ref[...]
    kr2_ref[...] = wk + pltpu.roll(wk, B_ROPE, axis=1)

    cqn = (cq * _rms_scale(cq) * gqa_ref[...]).astype(BF16)
    if transposed_q:
        g_nope = gq_ref[:B_NOPE, :] * (B_SCALE * LOG2E)
        g_rope = gq_ref[B_NOPE:, :] * csq_ref[...] * (B_SCALE * LOG2E)
        rows = Q_HEADS_PER_DOT * B_CAT
        for grp in range(B_HEADS // Q_HEADS_PER_DOT):
            qt = lax.dot_general(wuq_ref[grp * rows:(grp + 1) * rows, :], cqn, _NT, preferred_element_type=F32)
            for hh in range(Q_HEADS_PER_DOT):
                head = grp * Q_HEADS_PER_DOT + hh
                qn = qt[hh * B_CAT:hh * B_CAT + B_NOPE, :]
                qr = qt[hh * B_CAT + B_NOPE:(hh + 1) * B_CAT, :]
                sn = lax.rsqrt(jnp.mean(qn * qn, axis=0, keepdims=True) + EPS)
                sr = lax.rsqrt(jnp.mean(qr * qr, axis=0, keepdims=True) + EPS)
                q_ref[0, head, 0:B_NOPE, :] = (qn * sn * g_nope).astype(BF16)
                q_ref[0, head, B_NOPE:B_CAT, :] = (qr * sr * g_rope).astype(BF16)
    else:
        q = jnp.dot(cqn, wuq_ref[...], preferred_element_type=F32)
        g_nope = gq_ref[:, :B_NOPE] * (B_SCALE * LOG2E)
        g_rope = gq_ref[:, B_NOPE:] * csq_ref[...] * (B_SCALE * LOG2E)
        for hh in range(B_HEADS):
            qn = q[:, hh * B_CAT:hh * B_CAT + B_NOPE]
            qr = q[:, hh * B_CAT + B_NOPE:(hh + 1) * B_CAT]
            q_ref[:, hh * B_CAT:hh * B_CAT + B_NOPE] = (qn * _rms_scale(qn) * g_nope).astype(BF16)
            q_ref[:, hh * B_CAT + B_NOPE:(hh + 1) * B_CAT] = (qr * _rms_scale(qr) * g_rope).astype(BF16)

    for c in range(B_WIDTH // GATE_CHUNK):
        sl = slice(c * GATE_CHUNK, (c + 1) * GATE_CHUNK)
        g = jnp.dot(h, wg_ref[:, sl], preferred_element_type=F32)
        sg_ref[:, sl] = (g / (1.0 + jnp.exp(-g))).astype(BF16)


def _proj_b(x, ln, w1, wg, wuq, gqa, gkva, gq, gkr, cs, *, tm, batch=None):
    t = x.shape[0]
    n_pos_tiles = cs.shape[0] // tm
    transposed_q = batch is not None
    pos_spec = pl.BlockSpec((tm, 2 * B_ROPE), lambda i: (i % n_pos_tiles, 0))
    if transposed_q:
        tiles = t // batch // tm
        wuq = wuq.T
        gq = jnp.broadcast_to(gq.reshape(B_CAT, 1), (B_CAT, tm))
        csq = cs.T
        gq_spec = _const_spec((B_CAT, tm))
        csq_spec = pl.BlockSpec((2 * B_ROPE, tm), lambda i: (0, i % n_pos_tiles))
        q_spec = pl.BlockSpec((1, B_HEADS, B_CAT, tm), lambda i: (i // tiles, 0, 0, i % tiles))
        q_shape = jax.ShapeDtypeStruct((batch, B_HEADS, B_CAT, t // batch), BF16)
    else:
        csq = cs
        gq_spec = _const_spec((1, B_CAT))
        csq_spec = pos_spec
        q_spec = pl.BlockSpec((tm, B_HEADS * B_CAT), lambda i: (i, 0))
        q_shape = jax.ShapeDtypeStruct((t, B_HEADS * B_CAT), BF16)
    return pl.pallas_call(
        functools.partial(_proj_b_kernel, transposed_q=transposed_q),
        grid=(t // tm,),
        in_specs=[
            pl.BlockSpec((tm, D_MODEL), lambda i: (i, 0)),
            _const_spec((1, D_MODEL)),
            _const_spec(w1.shape),
            _const_spec(wg.shape),
            _const_spec(wuq.shape),
            _const_spec((1, B_Q_RANK)),
            _const_spec((1, B_KV_RANK)),
            gq_spec,
            _const_spec((1, 2 * B_ROPE)),
            pos_spec,
            csq_spec,
        ],
        out_specs=[
            q_spec,
            pl.BlockSpec((tm, B_KV_RANK), lambda i: (i, 0)),
            pl.BlockSpec((tm, 2 * B_ROPE), lambda i: (i, 0)),
            pl.BlockSpec((tm, B_WIDTH), lambda i: (i, 0)),
        ],
        out_shape=[
            q_shape,
            jax.ShapeDtypeStruct((t, B_KV_RANK), F32),
            jax.ShapeDtypeStruct((t, 2 * B_ROPE), F32),
            jax.ShapeDtypeStruct((t, B_WIDTH), BF16),
        ],
        compiler_params=pltpu.CompilerParams(
            dimension_semantics=("arbitrary",), vmem_limit_bytes=VMEM_LIMIT),
        name="proj_b",
    )(x, ln, w1, wg, wuq, gqa, gkva, gq, gkr, cs, csq)


def _expand_b_kernel(ckv_ref, kr2_ref, wuk_ref, wuv_ref, gkn_ref, kcat_ref, v_ref, *, transposed_v):
    c = ckv_ref[...].astype(BF16)
    kn = jnp.dot(c, wuk_ref[...], preferred_element_type=F32)
    krb = kr2_ref[...].astype(BF16)
    g = gkn_ref[...]
    for hh in range(B_HEADS):
        s = kn[:, hh * B_NOPE:(hh + 1) * B_NOPE]
        kcat_ref[:, hh * B_CAT:hh * B_CAT + B_NOPE] = (s * _rms_scale(s) * g).astype(BF16)
        kcat_ref[:, hh * B_CAT + B_NOPE:(hh + 1) * B_CAT] = krb
    if transposed_v:
        vt = lax.dot_general(wuv_ref[...], c, _NT, preferred_element_type=F32).astype(BF16)
        tm = c.shape[0]
        for hh in range(B_HEADS):
            v_ref[0, hh, 0, 0:B_V, :] = vt[hh * B_V:(hh + 1) * B_V, :]
            v_ref[0, hh, 0, B_V:VT_ROWS, :] = jnp.ones((VT_ROWS - B_V, tm), BF16)
    else:
        v_ref[...] = jnp.dot(c, wuv_ref[...], preferred_element_type=F32).astype(BF16)


def _expand_b(ckv, kr2, wuk, wuv, gkn, *, tm, batch=None):
    t = ckv.shape[0]
    transposed_v = batch is not None
    if transposed_v:
        tiles = t // batch // tm
        v_spec = pl.BlockSpec((1, B_HEADS, 1, VT_ROWS, tm), lambda i: (i // tiles, 0, i % tiles, 0, 0))
        v_shape = jax.ShapeDtypeStruct((batch, B_HEADS, tiles, VT_ROWS, tm), BF16)
    else:
        v_spec = pl.BlockSpec((tm, B_WIDTH), lambda i: (i, 0))
        v_shape = jax.ShapeDtypeStruct((t, B_WIDTH), BF16)
    return pl.pallas_call(
        functools.partial(_expand_b_kernel, transposed_v=transposed_v),
        grid=(t // tm,),
        in_specs=[
            pl.BlockSpec((tm, B_KV_RANK), lambda i: (i, 0)),
            pl.BlockSpec((tm, 2 * B_ROPE), lambda i: (i, 0)),
            _const_spec(wuk.shape),
            _const_spec(wuv.shape),
            _const_spec((1, B_NOPE)),
        ],
        out_specs=[pl.BlockSpec((tm, B_HEADS * B_CAT), lambda i: (i, 0)), v_spec],
        out_shape=[jax.ShapeDtypeStruct((t, B_HEADS * B_CAT), BF16), v_shape],
        compiler_params=pltpu.CompilerParams(
            dimension_semantics=("arbitrary",), vmem_limit_bytes=VMEM_LIMIT),
        name="expand_b",
    )(ckv, kr2, wuk, wuv, gkn)


def _attn_bt_kernel(q_ref, k_ref, vt_ref, o_ref, s_ref, p_ref, alpha_ref, mask_ref, m_ref, acc_ref,
                    *, tq, n_blocks):
    pairs = [(qi, kj) for qi in range(n_blocks) for kj in range(qi + 1)]

    key_chunk = lax.broadcasted_iota(jnp.int32, (tq, tq), 0) // CHUNK
    q_chunk = lax.broadcasted_iota(jnp.int32, (tq, tq), 1) // CHUNK
    mask_ref[...] = jnp.where(key_chunk <= q_chunk, 0.0, NEG_INF).astype(F32)

    half = tq // 2

    def scores(t):
        qi, kj = pairs[t]
        slot = t % 2
        kb = k_ref[0, kj * tq:(kj + 1) * tq, :]
        q_t = q_ref[0, 0, :, qi * tq:(qi + 1) * tq]
        if kj == qi:
            s_ref[slot, :half, :] = jnp.dot(kb[:half], q_t, preferred_element_type=F32)
            s_ref[slot, half:, half:] = jnp.dot(kb[half:], q_t[:, half:], preferred_element_type=F32)
        else:
            s_ref[slot] = jnp.dot(kb, q_t, preferred_element_type=F32)

    def probabilities(t):
        qi, kj = pairs[t]
        slot = t % 2
        if kj == qi:
            s_parts = [s_ref[slot, :half, :half] + mask_ref[:half, :half],
                       s_ref[slot, :, half:] + mask_ref[:, half:]]
        else:
            s_parts = [s_ref[slot]]
        col_max = jnp.concatenate([jnp.max(s, axis=0, keepdims=True) for s in s_parts], axis=1)
        if kj == 0:
            m_new = col_max
        else:
            m_new = jnp.maximum(m_ref[...], col_max)
            alpha_ref[slot] = jnp.exp2(m_ref[...] - m_new)
        m_ref[...] = m_new
        if kj == qi:
            p_ref[slot, :half, :half] = jnp.exp2(s_parts[0] - m_new[:, :half]).astype(BF16)
            p_ref[slot, :, half:] = jnp.exp2(s_parts[1] - m_new[:, half:]).astype(BF16)
        else:
            p_ref[slot] = jnp.exp2(s_parts[0] - m_new).astype(BF16)

    def values(t):
        qi, kj = pairs[t]
        slot = t % 2
        vt = vt_ref[0, 0, kj]
        if kj == qi:
            pv = jnp.concatenate(
                [jnp.dot(vt[:, :half], p_ref[slot, :half, :half], preferred_element_type=F32),
                 jnp.dot(vt, p_ref[slot, :, half:], preferred_element_type=F32)], axis=1)
        else:
            pv = jnp.dot(vt, p_ref[slot], preferred_element_type=F32)
        acc = pv if kj == 0 else alpha_ref[slot] * acc_ref[...] + pv
        if kj == qi:
            o_t = acc[:B_V, :] / acc[B_V:B_V + 1, :]
            o_ref[0, qi * tq:(qi + 1) * tq, :] = o_t.T.astype(BF16)
        else:
            acc_ref[...] = acc

    scores(0)
    for t in range(len(pairs) + 1):
        if t + 1 < len(pairs):
            scores(t + 1)
        if t < len(pairs):
            probabilities(t)
        if t >= 1:
            values(t - 1)


def _attn_bt(qt, kcat, vt, *, batch, seq, tq):
    k3 = kcat.reshape(batch, seq, B_HEADS * B_CAT)
    n_blocks = seq // tq
    out = pl.pallas_call(
        functools.partial(_attn_bt_kernel, tq=tq, n_blocks=n_blocks),
        grid=(batch, B_HEADS),
        in_specs=[
            pl.BlockSpec((1, 1, B_CAT, seq), lambda b, h: (b, h, 0, 0)),
            pl.BlockSpec((1, seq, B_CAT), lambda b, h: (b, 0, h)),
            pl.BlockSpec((1, 1, n_blocks, VT_ROWS, tq), lambda b, h: (b, h, 0, 0, 0)),
        ],
        out_specs=pl.BlockSpec((1, seq, B_V), lambda b, h: (b, 0, h)),
        out_shape=jax.ShapeDtypeStruct((batch, seq, B_WIDTH), BF16),
        scratch_shapes=[pltpu.VMEM((2, tq, tq), F32), pltpu.VMEM((2, tq, tq), BF16),
                        pltpu.VMEM((2, 1, tq), F32), pltpu.VMEM((tq, tq), F32),
                        pltpu.VMEM((1, tq), F32), pltpu.VMEM((VT_ROWS, tq), F32)],
        compiler_params=pltpu.CompilerParams(
            dimension_semantics=("arbitrary", "arbitrary"), vmem_limit_bytes=VMEM_LIMIT),
        name="attn_bt",
    )(qt, k3, vt)
    return out.reshape(batch * seq, B_WIDTH)


def _attn_b_kernel(q_ref, k_ref, v_ref, o_ref):
    for hh in range(B_HEADS):
        q = q_ref[0, :, hh * B_CAT:(hh + 1) * B_CAT]
        kb = k_ref[0, :, hh * B_CAT:(hh + 1) * B_CAT]
        s = lax.dot_general(q, kb, _NT, preferred_element_type=F32)
        p = jnp.exp2(s - jnp.max(s, axis=-1, keepdims=True))
        l = jnp.sum(p, axis=-1, keepdims=True)
        vb = v_ref[0, :, hh * B_V:(hh + 1) * B_V]
        o = jnp.dot(p.astype(BF16), vb, preferred_element_type=F32) / l
        o_ref[0, :, hh * B_V:(hh + 1) * B_V] = o.astype(BF16)


def _attn_b(qcat, kcat, v, *, batch, q_len, k_len):
    q3 = qcat.reshape(batch, q_len, B_HEADS * B_CAT)
    k3 = kcat.reshape(batch, k_len, B_HEADS * B_CAT)
    v3 = v.reshape(batch, k_len, B_WIDTH)
    out = pl.pallas_call(
        _attn_b_kernel,
        grid=(batch,),
        in_specs=[
            pl.BlockSpec((1, q_len, B_HEADS * B_CAT), lambda b: (b, 0, 0)),
            pl.BlockSpec((1, k_len, B_HEADS * B_CAT), lambda b: (b, 0, 0)),
            pl.BlockSpec((1, k_len, B_WIDTH), lambda b: (b, 0, 0)),
        ],
        out_specs=pl.BlockSpec((1, q_len, B_WIDTH), lambda b: (b, 0, 0)),
        out_shape=jax.ShapeDtypeStruct((batch, q_len, B_WIDTH), BF16),
        compiler_params=pltpu.CompilerParams(
            dimension_semantics=("arbitrary",), vmem_limit_bytes=VMEM_LIMIT),
        name="attn_b",
    )(q3, k3, v3)
    return out.reshape(batch * q_len, B_WIDTH)


def _swap_halves(a):
    half = a.shape[-1] // 2
    return jnp.concatenate([a[..., half:], a[..., :half]], axis=-1)


def _rope_table(pos):
    half = B_ROPE // 2
    inv = ROPE_THETA ** (-jnp.arange(half, dtype=F32) / half)
    ang = pos.astype(F32)[:, None] * inv[None, :]
    cos, sin = jnp.cos(ang), jnp.sin(ang)
    return jnp.concatenate([cos, cos, -sin, sin], axis=-1)


def _row(v):
    return v.reshape(1, -1).astype(F32)


def _layer_a(x, k_cache, v_cache, ln, w_in, gq, gk, table, w_out, *, batch, seq, tm, tq, tm_out):
    if k_cache is None:
        keep = min(A_WINDOW, seq)
        keep_rows, tiles_per_keep = keep, seq // tm
    else:
        keep = seq
        keep_rows, tiles_per_keep = tm, 1
    q, k, v, sg, k32, v32 = _proj_a(x, ln, w_in, gq, gk, tm=tm, ts=A_PROJ_SEG,
                                    keep_rows=keep_rows, tiles_per_keep=tiles_per_keep)
    bias = _band_bias(table, tq, mask_prefix=k_cache is None)
    o = _attn_a(q, k, v, bias, k_cache, v_cache, batch=batch, seq=seq, tq=tq,
                heads=1 if k_cache is None else A_HEADS)
    y = _out_proj(x, sg, o, w_out, tm=tm_out)
    shape = (batch, keep, A_HEADS, A_HEAD_DIM)
    return y, k32.reshape(shape), v32.reshape(shape)


def _layer_b(x, ckv_cache, kr_cache, pos, ln, w1, wg, wuq, gqa, gkva, gq, gkr, wuk, wuv, gkn, w_out,
             *, batch, seq, tm, tm_out):
    cs = _rope_table(pos)
    if ckv_cache is not None:
        cs = jnp.tile(cs, (batch, 1))
    qcat, ckv, kr2, sg = _proj_b(x, ln, w1, wg, wuq, gqa, gkva, gq, gkr, cs, tm=tm,
                                 batch=batch if ckv_cache is None else None)
    if ckv_cache is None:
        tq = 512
        kcat, vt = _expand_b(ckv, kr2, wuk, wuv.T, gkn, tm=tq, batch=batch)
        o = _attn_bt(qcat, kcat, vt, batch=batch, seq=seq, tq=tq)
    else:
        k_len = ckv_cache.shape[1] + seq
        ckv_all = jnp.concatenate(
            [ckv_cache, ckv.reshape(batch, seq, B_KV_RANK)], axis=1).reshape(batch * k_len, B_KV_RANK)
        kr2_all = jnp.concatenate(
            [jnp.concatenate([kr_cache, kr_cache], axis=-1), kr2.reshape(batch, seq, 2 * B_ROPE)],
            axis=1).reshape(batch * k_len, 2 * B_ROPE)
        kcat, v = _expand_b(ckv_all, kr2_all, wuk, wuv, gkn, tm=512)
        o = _attn_b(qcat, kcat, v, batch=batch, q_len=seq, k_len=k_len)
    y = _out_proj(x, sg, o, w_out, tm=tm_out)
    return y, ckv.reshape(batch, seq, B_KV_RANK), kr2[:, :B_ROPE].reshape(batch, seq, B_ROPE)


def kernel(x_prompt, x_sample, cache_a_k, cache_a_v, cache_b_ckv, cache_b_krope, a_ln, w_a_in, a_q_norm, a_k_norm, a_rel_bias, w_a_out, b_ln, w_b_in, b_q_a_norm, w_b_uq, b_kv_a_norm, w_b_uk, w_b_uv, b_q_nope_norm, b_k_nope_norm, b_q_rope_norm, b_k_rope_norm, w_b_out):
    batch, seq, _ = x_prompt.shape
    dec_batch, dec_seq, _ = x_sample.shape
    dec_tokens = dec_batch * dec_seq
    depth = a_ln.shape[0] + b_ln.shape[0]
    yp = x_prompt.reshape(batch * seq, D_MODEL)
    ys = x_sample.reshape(dec_tokens, D_MODEL)
    akp, avp, aks, avs, bcp, brp, bcs, brs = ([] for _ in range(8))
    for layer in range(depth):
        i = layer // 2
        if layer % 2 == 0:
            params = (_row(a_ln[i]), w_a_in[i].astype(BF16), _row(a_q_norm[i]), _row(a_k_norm[i]),
                      a_rel_bias[i], w_a_out[i].astype(BF16))
            yp, k_p, v_p = _layer_a(yp, None, None, *params, batch=batch, seq=seq,
                                    tm=1024, tq=4 * CHUNK, tm_out=512)
            ys, k_s, v_s = _layer_a(ys, cache_a_k[i], cache_a_v[i], *params, batch=dec_batch, seq=dec_seq,
                                    tm=dec_tokens, tq=CHUNK, tm_out=dec_tokens)
            akp.append(k_p); avp.append(v_p); aks.append(k_s); avs.append(v_s)
        else:
            w_in = w_b_in[i]
            c1, c2 = B_Q_RANK + B_KV_RANK, B_Q_RANK + B_KV_RANK + B_ROPE
            w_kr = w_in[:, c1:c2]
            w1 = jnp.concatenate([w_in[:, :c1], w_kr, _swap_halves(w_kr)], axis=1).astype(BF16)
            wg = w_in[:, c2:].astype(BF16)
            wuq = w_b_uq[i].reshape(B_Q_RANK, B_HEADS, B_QK)
            wuq_rope = wuq[..., B_NOPE:]
            wuq = jnp.concatenate([wuq[..., :B_NOPE], wuq_rope, _swap_halves(wuq_rope)],
                                  axis=-1).reshape(B_Q_RANK, B_HEADS * B_CAT).astype(BF16)
            g_qr, g_kr = b_q_rope_norm[i], b_k_rope_norm[i]
            gq = _row(jnp.concatenate([b_q_nope_norm[i], g_qr, _swap_halves(g_qr)]))
            gkr = _row(jnp.concatenate([g_kr, _swap_halves(g_kr)]))
            params = (_row(b_ln[i]), w1, wg, wuq, _row(b_q_a_norm[i]), _row(b_kv_a_norm[i]), gq, gkr,
                      w_b_uk[i].astype(BF16), w_b_uv[i].astype(BF16), _row(b_k_nope_norm[i]),
                      w_b_out[i].astype(BF16))
            yp, c_p, r_p = _layer_b(yp, None, None, jnp.arange(seq), *params, batch=batch, seq=seq,
                                    tm=512, tm_out=512)
            ys, c_s, r_s = _layer_b(ys, cache_b_ckv[i], cache_b_krope[i], PAST_LEN + jnp.arange(dec_seq),
                                    *params, batch=dec_batch, seq=dec_seq, tm=256, tm_out=dec_tokens)
            bcp.append(c_p); brp.append(r_p); bcs.append(c_s); brs.append(r_s)
    yp = yp.reshape(batch, seq, D_MODEL)
    ys = ys.reshape(dec_batch, dec_seq, D_MODEL)
    return (yp, ys,
            jnp.stack(akp), jnp.stack(avp), jnp.stack(bcp), jnp.stack(brp),
            jnp.stack(aks), jnp.stack(avs), jnp.stack(bcs), jnp.stack(brs))
```

```python
import functools
import math

import numpy as np

import jax
import jax.numpy as jnp
from jax import lax
from jax.experimental import pallas as pl
from jax.experimental.pallas import tpu as pltpu

D_MODEL = 2048
CHUNK = 64
EPS = 1e-6
NEG_INF = -1e30
PAST_LEN = 1024
LOG2E = math.log2(math.e)

A_HEADS = 16
A_HEAD_DIM = 128
A_WIDTH = A_HEADS * A_HEAD_DIM
A_WINDOW = 8 * CHUNK
A_BAND = A_WINDOW + CHUNK
REL_CLIP = 128
A_SCALE = A_HEAD_DIM ** -0.5
A_PROJ_SEG = 256

B_HEADS = 16
B_NOPE = 128
B_ROPE = 64
B_V = 128
B_QK = B_NOPE + B_ROPE
B_Q_RANK = 512
B_KV_RANK = 512
B_WIDTH = B_HEADS * B_V
B_SCALE = B_QK ** -0.5
ROPE_THETA = 10000.0
B_CAT = B_NOPE + 2 * B_ROPE
BF16_SUBLANES = 16
VT_ROWS = B_V + BF16_SUBLANES

LANES = 128
VMEM_LIMIT = 56 * 1024 * 1024

F32 = jnp.float32
BF16 = jnp.bfloat16
_NT = (((1,), (1,)), ((), ()))


def _rms_scale(x):
    return lax.rsqrt(jnp.mean(x * x, axis=-1, keepdims=True) + EPS)


def _const_spec(shape):
    return pl.BlockSpec(shape, lambda *_: (0,) * len(shape), pipeline_mode=pl.Buffered(1))


def _proj_a_kernel(x_ref, ln_ref, wq_ref, wk_ref, wv_ref, wg_ref, gq_ref, gk_ref,
                   q_ref, k_ref, v_ref, sg_ref, k32_ref, v32_ref, h_ref, *, ts, tiles_per_keep, keep_rows):
    i = pl.program_id(0)
    j = pl.program_id(1)
    tm = x_ref.shape[0]

    @pl.when(j == 0)
    def _():
        x = x_ref[...]
        h_ref[...] = (x * _rms_scale(x) * ln_ref[...]).astype(BF16)

    h = h_ref[...]
    q = jnp.dot(h, wq_ref[...], preferred_element_type=F32)
    gq = gq_ref[...] * (A_SCALE * LOG2E)
    for hh in range(ts // LANES):
        sl = slice(hh * LANES, (hh + 1) * LANES)
        s = q[:, sl]
        q_ref[:, sl] = (s * _rms_scale(s) * gq).astype(BF16)

    k = jnp.dot(h, wk_ref[...], preferred_element_type=F32)
    gk = gk_ref[...]
    k_tails = []
    for hh in range(ts // LANES):
        sl = slice(hh * LANES, (hh + 1) * LANES)
        s = k[:, sl]
        kn = s * _rms_scale(s) * gk
        k_ref[:, sl] = kn.astype(BF16)
        k_tails.append(kn[tm - keep_rows:, :])

    v = jnp.dot(h, wv_ref[...], preferred_element_type=F32)
    v_ref[...] = v.astype(BF16)
    g = jnp.dot(h, wg_ref[...], preferred_element_type=F32)
    sg_ref[...] = (g / (1.0 + jnp.exp(-g))).astype(BF16)

    @pl.when((i % tiles_per_keep) == (tiles_per_keep - 1))
    def _():
        for hh, kn_tail in enumerate(k_tails):
            k32_ref[:, hh * LANES:(hh + 1) * LANES] = kn_tail
        v32_ref[...] = v[tm - keep_rows:, :]


def _proj_a(x, ln, w, gq, gk, *, tm, ts, keep_rows, tiles_per_keep):
    t = x.shape[0]
    nm = t // tm
    nkeep = nm // tiles_per_keep
    nj = A_WIDTH // ts

    def w_spec(part):
        return pl.BlockSpec((D_MODEL, ts), lambda i, j: (0, part * nj + j))

    def tail_map(i, j):
        is_tail = (i % tiles_per_keep) == (tiles_per_keep - 1)
        return (i // tiles_per_keep, jnp.where(is_tail, j, 0))

    kernel = functools.partial(_proj_a_kernel, ts=ts, tiles_per_keep=tiles_per_keep, keep_rows=keep_rows)
    act = jax.ShapeDtypeStruct((t, A_WIDTH), BF16)
    tail = jax.ShapeDtypeStruct((nkeep * keep_rows, A_WIDTH), F32)
    return pl.pallas_call(
        kernel,
        grid=(nm, nj),
        in_specs=[
            pl.BlockSpec((tm, D_MODEL), lambda i, j: (i, 0)),
            _const_spec((1, D_MODEL)),
            w_spec(0), w_spec(1), w_spec(2), w_spec(3),
            _const_spec((1, A_HEAD_DIM)),
            _const_spec((1, A_HEAD_DIM)),
        ],
        out_specs=[pl.BlockSpec((tm, ts), lambda i, j: (i, j))] * 4
                  + [pl.BlockSpec((keep_rows, ts), tail_map)] * 2,
        out_shape=[act] * 4 + [tail] * 2,
        scratch_shapes=[pltpu.VMEM((tm, D_MODEL), BF16)],
        compiler_params=pltpu.CompilerParams(
            dimension_semantics=("arbitrary", "arbitrary"), vmem_limit_bytes=VMEM_LIMIT),
        name="proj_a",
    )(x, ln, w, w, w, w, gq, gk)


def _attn_a_kernel(*refs, has_cache, tq, seq, n_bias, heads):
    if has_cache:
        q_ref, k_ref, v_ref, kc_ref, vc_ref, bias_ref, o_ref, kpad, vpad, s_ref = refs
    else:
        q_ref, k_ref, v_ref, bias_ref, o_ref, kpad, vpad, s_ref = refs
    width = A_WINDOW + tq
    n_blocks = seq // tq

    for hh in range(heads):
        cols = slice(hh * A_HEAD_DIM, (hh + 1) * A_HEAD_DIM)
        if has_cache:
            kpad[0:A_WINDOW, :] = kc_ref[0, hh]
            vpad[0:A_WINDOW, :] = vc_ref[0, hh]
        else:
            kpad[0:A_WINDOW, :] = jnp.zeros((A_WINDOW, A_HEAD_DIM), BF16)
            vpad[0:A_WINDOW, :] = jnp.zeros((A_WINDOW, A_HEAD_DIM), BF16)
        kpad[A_WINDOW:A_WINDOW + seq, :] = k_ref[0, :, cols]
        vpad[A_WINDOW:A_WINDOW + seq, :] = v_ref[0, :, cols]

        def scores(t):
            q = q_ref[0, t * tq:(t + 1) * tq, cols]
            kb = kpad[t * tq:t * tq + width, :]
            s_ref[t % 2] = lax.dot_general(q, kb, _NT, preferred_element_type=F32)

        scores(0)
        for t in range(n_blocks):
            if t + 1 < n_blocks:
                scores(t + 1)
            s = s_ref[t % 2] + bias_ref[hh, min(t, n_bias - 1)]
            m = jnp.max(s, axis=-1, keepdims=True)
            p = jnp.exp2(s - m)
            l = jnp.sum(p, axis=-1, keepdims=True)
            vb = vpad[t * tq:t * tq + width, :]
            o = jnp.dot(p.astype(BF16), vb, preferred_element_type=F32) / l
            o_ref[0, t * tq:(t + 1) * tq, cols] = o.astype(BF16)


def _attn_a(q, k, v, bias, k_cache, v_cache, *, batch, seq, tq, heads):
    has_cache = k_cache is not None
    width = A_WINDOW + tq
    n_bias = bias.shape[1]
    in_specs = [pl.BlockSpec((1, seq, heads * A_HEAD_DIM), lambda b, h: (b, 0, h))] * 3
    args = [a.reshape(batch, seq, A_WIDTH) for a in (q, k, v)]
    if has_cache:
        in_specs += [pl.BlockSpec((1, heads, A_WINDOW, A_HEAD_DIM), lambda b, h: (b, h, 0, 0))] * 2
        args += [k_cache, v_cache]
    in_specs.append(pl.BlockSpec((heads, n_bias, tq, width), lambda b, h: (h, 0, 0, 0)))
    args.append(bias)
    kernel = functools.partial(_attn_a_kernel, has_cache=has_cache, tq=tq, seq=seq, n_bias=n_bias, heads=heads)
    out = pl.pallas_call(
        kernel,
        grid=(batch, A_HEADS // heads),
        in_specs=in_specs,
        out_specs=pl.BlockSpec((1, seq, heads * A_HEAD_DIM), lambda b, h: (b, 0, h)),
        out_shape=jax.ShapeDtypeStruct((batch, seq, A_WIDTH), BF16),
        scratch_shapes=[pltpu.VMEM((A_WINDOW + seq, A_HEAD_DIM), BF16),
                        pltpu.VMEM((A_WINDOW + seq, A_HEAD_DIM), BF16),
                        pltpu.VMEM((2, tq, width), F32)],
        compiler_params=pltpu.CompilerParams(
            dimension_semantics=("arbitrary", "arbitrary"), vmem_limit_bytes=VMEM_LIMIT),
        name="attn_a",
    )(*args)
    return out.reshape(batch * seq, A_WIDTH)


def _band_bias(table, tq, mask_prefix):
    width = A_WINDOW + tq
    length = CHUNK + A_BAND - 1
    n_table = 2 * REL_CLIP + 1
    c0 = A_WINDOW + REL_CLIP - (A_BAND - 1)
    n_lo, first = max(0, -c0), max(0, c0)
    n_hi = length - n_lo - (n_table - first)
    heads = table.shape[0]
    v = jnp.concatenate([jnp.broadcast_to(table[:, :1], (heads, n_lo)), table[:, first:],
                         jnp.broadcast_to(table[:, -1:], (heads, n_hi))], axis=1).astype(F32)
    w = jnp.pad(v[:, ::-1], ((0, 0), (0, 1)))
    skew = jnp.tile(w, (1, CHUNK))[:, :CHUNK * length].reshape(heads, CHUNK, length)
    band = skew[:, :, CHUNK - 1:CHUNK - 1 + A_BAND] * LOG2E
    chunks = [jnp.pad(band, ((0, 0), (0, 0), (c * CHUNK, tq - CHUNK - c * CHUNK)), constant_values=NEG_INF)
              for c in range(tq // CHUNK)]
    block = jnp.concatenate(chunks, axis=1)
    if not mask_prefix:
        return block[:, None]
    j = np.arange(width)[None, :]
    keep = np.stack([np.broadcast_to(j + v_idx * tq >= A_WINDOW, (tq, width))
                     for v_idx in range(A_WINDOW // tq + 1)])
    return jnp.where(keep[None], block[:, None], NEG_INF)


def _out_proj_kernel(x_ref, sg_ref, o_ref, w_ref, y_ref):
    gated = sg_ref[...] * o_ref[...]
    y_ref[...] = x_ref[...] + jnp.dot(gated, w_ref[...], preferred_element_type=F32)


def _out_proj(x, sg, o, w, *, tm):
    t, width = o.shape
    return pl.pallas_call(
        _out_proj_kernel,
        grid=(t // tm,),
        in_specs=[
            pl.BlockSpec((tm, D_MODEL), lambda i: (i, 0)),
            pl.BlockSpec((tm, width), lambda i: (i, 0)),
            pl.BlockSpec((tm, width), lambda i: (i, 0)),
            _const_spec((width, D_MODEL)),
        ],
        out_specs=pl.BlockSpec((tm, D_MODEL), lambda i: (i, 0)),
        out_shape=jax.ShapeDtypeStruct((t, D_MODEL), F32),
        compiler_params=pltpu.CompilerParams(
            dimension_semantics=("arbitrary",), vmem_limit_bytes=VMEM_LIMIT),
        name="out_proj",
    )(x, sg, o, w)


Q_HEADS_PER_DOT = 4
GATE_CHUNK = 512


def _proj_b_kernel(x_ref, ln_ref, w1_ref, wg_ref, wuq_ref, gqa_ref, gkva_ref, gq_ref, gkr_ref, cs_ref, csq_ref,
                   q_ref, ckv_ref, kr2_ref, sg_ref, *, transposed_q):
    x = x_ref[...]
    h = (x * _rms_scale(x) * ln_ref[...]).astype(BF16)
    t1 = jnp.dot(h, w1_ref[...], preferred_element_type=F32)
    cq = t1[:, :B_Q_RANK]
    ckv = t1[:, B_Q_RANK:B_Q_RANK + B_KV_RANK]
    kr = t1[:, B_Q_RANK + B_KV_RANK:]

    ckv_ref[...] = ckv * _rms_scale(ckv) * gkva_ref[...]
    wk = kr * _rms_scale(kr) * gkr_ref[...] * cs_ref[...]
    kr2_ref[...] = wk + pltpu.roll(wk, B_ROPE, axis=1)

    cqn = (cq * _rms_scale(cq) * gqa_ref[...]).astype(BF16)
    if transposed_q:
        g_nope = gq_ref[:B_NOPE, :] * (B_SCALE * LOG2E)
        g_rope = gq_ref[B_NOPE:, :] * csq_ref[...] * (B_SCALE * LOG2E)
        rows = Q_HEADS_PER_DOT * B_CAT
        for grp in range(B_HEADS // Q_HEADS_PER_DOT):
            qt = lax.dot_general(wuq_ref[grp * rows:(grp + 1) * rows, :], cqn, _NT, preferred_element_type=F32)
            for hh in range(Q_HEADS_PER_DOT):
                head = grp * Q_HEADS_PER_DOT + hh
                qn = qt[hh * B_CAT:hh * B_CAT + B_NOPE, :]
                qr = qt[hh * B_CAT + B_NOPE:(hh + 1) * B_CAT, :]
                sn = lax.rsqrt(jnp.mean(qn * qn, axis=0, keepdims=True) + EPS)
                sr = lax.rsqrt(jnp.mean(qr * qr, axis=0, keepdims=True) + EPS)
                q_ref[0, head, 0:B_NOPE, :] = (qn * sn * g_nope).astype(BF16)
                q_ref[0, head, B_NOPE:B_CAT, :] = (qr * sr * g_rope).astype(BF16)
    else:
        q = jnp.dot(cqn, wuq_ref[...], preferred_element_type=F32)
        g_nope = gq_ref[:, :B_NOPE] * (B_SCALE * LOG2E)
        g_rope = gq_ref[:, B_NOPE:] * csq_ref[...] * (B_SCALE * LOG2E)
        for hh in range(B_HEADS):
            qn = q[:, hh * B_CAT:hh * B_CAT + B_NOPE]
            qr = q[:, hh * B_CAT + B_NOPE:(hh + 1) * B_CAT]
            q_ref[:, hh * B_CAT:hh * B_CAT + B_NOPE] = (qn * _rms_scale(qn) * g_nope).astype(BF16)
            q_ref[:, hh * B_CAT + B_NOPE:(hh + 1) * B_CAT] = (qr * _rms_scale(qr) * g_rope).astype(BF16)

    for c in range(B_WIDTH // GATE_CHUNK):
        sl = slice(c * GATE_CHUNK, (c + 1) * GATE_CHUNK)
        g = jnp.dot(h, wg_ref[:, sl], preferred_element_type=F32)
        sg_ref[:, sl] = (g / (1.0 + jnp.exp(-g))).astype(BF16)


def _proj_b(x, ln, w1, wg, wuq, gqa, gkva, gq, gkr, cs, *, tm, batch=None):
    t = x.shape[0]
    n_pos_tiles = cs.shape[0] // tm
    transposed_q = batch is not None
    pos_spec = pl.BlockSpec((tm, 2 * B_ROPE), lambda i: (i % n_pos_tiles, 0))
    if transposed_q:
        tiles = t // batch // tm
        wuq = wuq.T
        gq = jnp.broadcast_to(gq.reshape(B_CAT, 1), (B_CAT, tm))
        csq = cs.T
        gq_spec = _const_spec((B_CAT, tm))
        csq_spec = pl.BlockSpec((2 * B_ROPE, tm), lambda i: (0, i % n_pos_tiles))
        q_spec = pl.BlockSpec((1, B_HEADS, B_CAT, tm), lambda i: (i // tiles, 0, 0, i % tiles))
        q_shape = jax.ShapeDtypeStruct((batch, B_HEADS, B_CAT, t // batch), BF16)
    else:
        csq = cs
        gq_spec = _const_spec((1, B_CAT))
        csq_spec = pos_spec
        q_spec = pl.BlockSpec((tm, B_HEADS * B_CAT), lambda i: (i, 0))
        q_shape = jax.ShapeDtypeStruct((t, B_HEADS * B_CAT), BF16)
    return pl.pallas_call(
        functools.partial(_proj_b_kernel, transposed_q=transposed_q),
        grid=(t // tm,),
        in_specs=[
            pl.BlockSpec((tm, D_MODEL), lambda i: (i, 0)),
            _const_spec((1, D_MODEL)),
            _const_spec(w1.shape),
            _const_spec(wg.shape),
            _const_spec(wuq.shape),
            _const_spec((1, B_Q_RANK)),
            _const_spec((1, B_KV_RANK)),
            gq_spec,
            _const_spec((1, 2 * B_ROPE)),
            pos_spec,
            csq_spec,
        ],
        out_specs=[
            q_spec,
            pl.BlockSpec((tm, B_KV_RANK), lambda i: (i, 0)),
            pl.BlockSpec((tm, 2 * B_ROPE), lambda i: (i, 0)),
            pl.BlockSpec((tm, B_WIDTH), lambda i: (i, 0)),
        ],
        out_shape=[
            q_shape,
            jax.ShapeDtypeStruct((t, B_KV_RANK), F32),
            jax.ShapeDtypeStruct((t, 2 * B_ROPE), F32),
            jax.ShapeDtypeStruct((t, B_WIDTH), BF16),
        ],
        compiler_params=pltpu.CompilerParams(
            dimension_semantics=("arbitrary",), vmem_limit_bytes=VMEM_LIMIT),
        name="proj_b",
    )(x, ln, w1, wg, wuq, gqa, gkva, gq, gkr, cs, csq)


def _expand_b_kernel(ckv_ref, kr2_ref, wuk_ref, wuv_ref, gkn_ref, kcat_ref, v_ref, *, transposed_v):
    c = ckv_ref[...].astype(BF16)
    kn = jnp.dot(c, wuk_ref[...], preferred_element_type=F32)
    krb = kr2_ref[...].astype(BF16)
    g = gkn_ref[...]
    for hh in range(B_HEADS):
        s = kn[:, hh * B_NOPE:(hh + 1) * B_NOPE]
        kcat_ref[:, hh * B_CAT:hh * B_CAT + B_NOPE] = (s * _rms_scale(s) * g).astype(BF16)
        kcat_ref[:, hh * B_CAT + B_NOPE:(hh + 1) * B_CAT] = krb
    if transposed_v:
        vt = lax.dot_general(wuv_ref[...], c, _NT, preferred_element_type=F32).astype(BF16)
        tm = c.shape[0]
        for hh in range(B_HEADS):
            v_ref[0, hh, 0, 0:B_V, :] = vt[hh * B_V:(hh + 1) * B_V, :]
            v_ref[0, hh, 0, B_V:VT_ROWS, :] = jnp.ones((VT_ROWS - B_V, tm), BF16)
    else:
        v_ref[...] = jnp.dot(c, wuv_ref[...], preferred_element_type=F32).astype(BF16)


def _expand_b(ckv, kr2, wuk, wuv, gkn, *, tm, batch=None):
    t = ckv.shape[0]
    transposed_v = batch is not None
    if transposed_v:
        tiles = t // batch // tm
        v_spec = pl.BlockSpec((1, B_HEADS, 1, VT_ROWS, tm), lambda i: (i // tiles, 0, i % tiles, 0, 0))
        v_shape = jax.ShapeDtypeStruct((batch, B_HEADS, tiles, VT_ROWS, tm), BF16)
    else:
        v_spec = pl.BlockSpec((tm, B_WIDTH), lambda i: (i, 0))
        v_shape = jax.ShapeDtypeStruct((t, B_WIDTH), BF16)
    return pl.pallas_call(
        functools.partial(_expand_b_kernel, transposed_v=transposed_v),
        grid=(t // tm,),
        in_specs=[
            pl.BlockSpec((tm, B_KV_RANK), lambda i: (i, 0)),
            pl.BlockSpec((tm, 2 * B_ROPE), lambda i: (i, 0)),
            _const_spec(wuk.shape),
            _const_spec(wuv.shape),
            _const_spec((1, B_NOPE)),
        ],
        out_specs=[pl.BlockSpec((tm, B_HEADS * B_CAT), lambda i: (i, 0)), v_spec],
        out_shape=[jax.ShapeDtypeStruct((t, B_HEADS * B_CAT), BF16), v_shape],
        compiler_params=pltpu.CompilerParams(
            dimension_semantics=("arbitrary",), vmem_limit_bytes=VMEM_LIMIT),
        name="expand_b",
    )(ckv, kr2, wuk, wuv, gkn)


def _attn_bt_kernel(q_ref, k_ref, vt_ref, o_ref, s_ref, p_ref, alpha_ref, mask_ref, m_ref, acc_ref,
                    *, tq, n_blocks, kv_blocks):
    pairs = [(qi, k0, min(kv_blocks, qi + 1 - k0))
             for qi in range(n_blocks) for k0 in range(0, qi + 1, kv_blocks)]

    key_chunk = lax.broadcasted_iota(jnp.int32, (tq, tq), 0) // CHUNK
    q_chunk = lax.broadcasted_iota(jnp.int32, (tq, tq), 1) // CHUNK
    mask_ref[...] = jnp.where(key_chunk <= q_chunk, 0.0, NEG_INF).astype(F32)

    half = tq // 2

    def scores(t):
        qi, k0, nk = pairs[t]
        slot = t % 2
        q_t = q_ref[0, 0, :, qi * tq:(qi + 1) * tq]
        rows = nk * tq
        kb = k_ref[0, k0 * tq:k0 * tq + rows, :]
        if k0 + nk - 1 == qi:
            full = rows - half
            s_ref[slot, :full, :] = jnp.dot(kb[:full], q_t, preferred_element_type=F32)
            s_ref[slot, full:rows, half:] = jnp.dot(kb[full:], q_t[:, half:], preferred_element_type=F32)
        else:
            s_ref[slot, :rows, :] = jnp.dot(kb, q_t, preferred_element_type=F32)

    def regions(t):
        qi, k0, nk = pairs[t]
        rows = nk * tq
        if k0 + nk - 1 != qi:
            return [(0, rows, 0, tq, None)]
        base = rows - tq
        out = [(0, base, 0, tq, None)] if base else []
        return out + [(base, base + half, 0, half, 0), (base, base + half, half, tq, None),
                      (base + half, rows, half, tq, half)]

    def probabilities(t):
        qi, k0, nk = pairs[t]
        slot = t % 2
        parts = []
        for r0, r1, c0, c1, mrow in regions(t):
            s = s_ref[slot, r0:r1, c0:c1]
            if mrow is not None:
                s = s + mask_ref[mrow:mrow + (r1 - r0), c0:c1]
            parts.append(s)
        col_max = jnp.full((1, tq), NEG_INF, F32)
        for (r0, r1, c0, c1, _), s in zip(regions(t), parts):
            part_max = jnp.max(s, axis=0, keepdims=True)
            if c1 - c0 < tq:
                pad = jnp.full((1, tq - (c1 - c0)), NEG_INF, F32)
                part_max = jnp.concatenate([part_max, pad] if c0 == 0 else [pad, part_max], axis=1)
            col_max = jnp.maximum(col_max, part_max)
        if k0 == 0:
            m_new = col_max
        else:
            m_new = jnp.maximum(m_ref[...], col_max)
            alpha_ref[slot] = jnp.exp2(m_ref[...] - m_new)
        m_ref[...] = m_new
        for (r0, r1, c0, c1, _), s in zip(regions(t), parts):
            p_ref[slot, r0:r1, c0:c1] = jnp.exp2(s - m_new[:, c0:c1]).astype(BF16)

    def values(t):
        qi, k0, nk = pairs[t]
        slot = t % 2
        diag = k0 + nk - 1 == qi

        def product(c0, c1, last_rows):
            total = None
            for b in range(nk):
                n = last_rows if b == nk - 1 else tq
                term = jnp.dot(vt_ref[0, 0, k0 + b, :, :n], p_ref[slot, b * tq:b * tq + n, c0:c1],
                               preferred_element_type=F32)
                total = term if total is None else total + term
            return total

        if diag:
            pv = jnp.concatenate([product(0, half, half), product(half, tq, tq)], axis=1)
        else:
            pv = product(0, tq, tq)
        acc = pv if k0 == 0 else alpha_ref[slot] * acc_ref[...] + pv
        if diag:
            o_t = acc[:B_V, :] / acc[B_V:B_V + 1, :]
            o_ref[0, qi * tq:(qi + 1) * tq, :] = o_t.T.astype(BF16)
        else:
            acc_ref[...] = acc

    scores(0)
    for t in range(len(pairs) + 1):
        if t + 1 < len(pairs):
            scores(t + 1)
        if t < len(pairs):
            probabilities(t)
        if t >= 1:
            values(t - 1)


def _attn_bt(qt, kcat, vt, *, batch, seq, tq, kv_blocks):
    k3 = kcat.reshape(batch, seq, B_HEADS * B_CAT)
    n_blocks = seq // tq
    out = pl.pallas_call(
        functools.partial(_attn_bt_kernel, tq=tq, n_blocks=n_blocks, kv_blocks=kv_blocks),
        grid=(batch, B_HEADS),
        in_specs=[
            pl.BlockSpec((1, 1, B_CAT, seq), lambda b, h: (b, h, 0, 0)),
            pl.BlockSpec((1, seq, B_CAT), lambda b, h: (b, 0, h)),
            pl.BlockSpec((1, 1, n_blocks, VT_ROWS, tq), lambda b, h: (b, h, 0, 0, 0)),
        ],
        out_specs=pl.BlockSpec((1, seq, B_V), lambda b, h: (b, 0, h)),
        out_shape=jax.ShapeDtypeStruct((batch, seq, B_WIDTH), BF16),
        scratch_shapes=[pltpu.VMEM((2, kv_blocks * tq, tq), F32), pltpu.VMEM((2, kv_blocks * tq, tq), BF16),
                        pltpu.VMEM((2, 1, tq), F32), pltpu.VMEM((tq, tq), F32),
                        pltpu.VMEM((1, tq), F32), pltpu.VMEM((VT_ROWS, tq), F32)],
        compiler_params=pltpu.CompilerParams(
            dimension_semantics=("arbitrary", "arbitrary"), vmem_limit_bytes=VMEM_LIMIT),
        name="attn_bt",
    )(qt, k3, vt)
    return out.reshape(batch * seq, B_WIDTH)


def _attn_b_kernel(q_ref, k_ref, v_ref, o_ref):
    for hh in range(B_HEADS):
        q = q_ref[0, :, hh * B_CAT:(hh + 1) * B_CAT]
        kb = k_ref[0, :, hh * B_CAT:(hh + 1) * B_CAT]
        s = lax.dot_general(q, kb, _NT, preferred_element_type=F32)
        p = jnp.exp2(s - jnp.max(s, axis=-1, keepdims=True))
        l = jnp.sum(p, axis=-1, keepdims=True)
        vb = v_ref[0, :, hh * B_V:(hh + 1) * B_V]
        o = jnp.dot(p.astype(BF16), vb, preferred_element_type=F32) / l
        o_ref[0, :, hh * B_V:(hh + 1) * B_V] = o.astype(BF16)


def _attn_b(qcat, kcat, v, *, batch, q_len, k_len):
    q3 = qcat.reshape(batch, q_len, B_HEADS * B_CAT)
    k3 = kcat.reshape(batch, k_len, B_HEADS * B_CAT)
    v3 = v.reshape(batch, k_len, B_WIDTH)
    out = pl.pallas_call(
        _attn_b_kernel,
        grid=(batch,),
        in_specs=[
            pl.BlockSpec((1, q_len, B_HEADS * B_CAT), lambda b: (b, 0, 0)),
            pl.BlockSpec((1, k_len, B_HEADS * B_CAT), lambda b: (b, 0, 0)),
            pl.BlockSpec((1, k_len, B_WIDTH), lambda b: (b, 0, 0)),
        ],
        out_specs=pl.BlockSpec((1, q_len, B_WIDTH), lambda b: (b, 0, 0)),
        out_shape=jax.ShapeDtypeStruct((batch, q_len, B_WIDTH), BF16),
        compiler_params=pltpu.CompilerParams(
            dimension_semantics=("arbitrary",), vmem_limit_bytes=VMEM_LIMIT),
        name="attn_b",
    )(q3, k3, v3)
    return out.reshape(batch * q_len, B_WIDTH)


def _swap_halves(a):
    half = a.shape[-1] // 2
    return jnp.concatenate([a[..., half:], a[..., :half]], axis=-1)


def _rope_table(pos):
    half = B_ROPE // 2
    inv = ROPE_THETA ** (-jnp.arange(half, dtype=F32) / half)
    ang = pos.astype(F32)[:, None] * inv[None, :]
    cos, sin = jnp.cos(ang), jnp.sin(ang)
    return jnp.concatenate([cos, cos, -sin, sin], axis=-1)


def _row(v):
    return v.reshape(1, -1).astype(F32)


def _layer_a(x, k_cache, v_cache, ln, w_in, gq, gk, table, w_out, *, batch, seq, tm, tq, tm_out):
    if k_cache is None:
        keep = min(A_WINDOW, seq)
        keep_rows, tiles_per_keep = keep, seq // tm
    else:
        keep = seq
        keep_rows, tiles_per_keep = tm, 1
    q, k, v, sg, k32, v32 = _proj_a(x, ln, w_in, gq, gk, tm=tm, ts=A_PROJ_SEG,
                                    keep_rows=keep_rows, tiles_per_keep=tiles_per_keep)
    bias = _band_bias(table, tq, mask_prefix=k_cache is None)
    o = _attn_a(q, k, v, bias, k_cache, v_cache, batch=batch, seq=seq, tq=tq,
                heads=1 if k_cache is None else A_HEADS)
    y = _out_proj(x, sg, o, w_out, tm=tm_out)
    shape = (batch, keep, A_HEADS, A_HEAD_DIM)
    return y, k32.reshape(shape), v32.reshape(shape)


def _layer_b(x, ckv_cache, kr_cache, pos, ln, w1, wg, wuq, gqa, gkva, gq, gkr, wuk, wuv, gkn, w_out,
             *, batch, seq, tm, tm_out):
    cs = _rope_table(pos)
    if ckv_cache is not None:
        cs = jnp.tile(cs, (batch, 1))
    qcat, ckv, kr2, sg = _proj_b(x, ln, w1, wg, wuq, gqa, gkva, gq, gkr, cs, tm=tm,
                                 batch=batch if ckv_cache is None else None)
    if ckv_cache is None:
        tq = 512
        kcat, vt = _expand_b(ckv, kr2, wuk, wuv.T, gkn, tm=tq, batch=batch)
        o = _attn_bt(qcat, kcat, vt, batch=batch, seq=seq, tq=tq, kv_blocks=2)
    else:
        k_len = ckv_cache.shape[1] + seq
        ckv_all = jnp.concatenate(
            [ckv_cache, ckv.reshape(batch, seq, B_KV_RANK)], axis=1).reshape(batch * k_len, B_KV_RANK)
        kr2_all = jnp.concatenate(
            [jnp.concatenate([kr_cache, kr_cache], axis=-1), kr2.reshape(batch, seq, 2 * B_ROPE)],
            axis=1).reshape(batch * k_len, 2 * B_ROPE)
        kcat, v = _expand_b(ckv_all, kr2_all, wuk, wuv, gkn, tm=512)
        o = _attn_b(qcat, kcat, v, batch=batch, q_len=seq, k_len=k_len)
    y = _out_proj(x, sg, o, w_out, tm=tm_out)
    return y, ckv.reshape(batch, seq, B_KV_RANK), kr2[:, :B_ROPE].reshape(batch, seq, B_ROPE)


def kernel(x_prompt, x_sample, cache_a_k, cache_a_v, cache_b_ckv, cache_b_krope, a_ln, w_a_in, a_q_norm, a_k_norm, a_rel_bias, w_a_out, b_ln, w_b_in, b_q_a_norm, w_b_uq, b_kv_a_norm, w_b_uk, w_b_uv, b_q_nope_norm, b_k_nope_norm, b_q_rope_norm, b_k_rope_norm, w_b_out):
    batch, seq, _ = x_prompt.shape
    dec_batch, dec_seq, _ = x_sample.shape
    dec_tokens = dec_batch * dec_seq
    depth = a_ln.shape[0] + b_ln.shape[0]
    yp = x_prompt.reshape(batch * seq, D_MODEL)
    ys = x_sample.reshape(dec_tokens, D_MODEL)
    akp, avp, aks, avs, bcp, brp, bcs, brs = ([] for _ in range(8))
    for layer in range(depth):
        i = layer // 2
        if layer % 2 == 0:
            params = (_row(a_ln[i]), w_a_in[i].astype(BF16), _row(a_q_norm[i]), _row(a_k_norm[i]),
                      a_rel_bias[i], w_a_out[i].astype(BF16))
            yp, k_p, v_p = _layer_a(yp, None, None, *params, batch=batch, seq=seq,
                                    tm=1024, tq=4 * CHUNK, tm_out=512)
            kc = cache_a_k[i].transpose(0, 2, 1, 3).astype(BF16)
            vc = cache_a_v[i].transpose(0, 2, 1, 3).astype(BF16)
            ys, k_s, v_s = _layer_a(ys, kc, vc, *params, batch=dec_batch, seq=dec_seq,
                                    tm=dec_tokens, tq=CHUNK, tm_out=dec_tokens)
            akp.append(k_p); avp.append(v_p); aks.append(k_s); avs.append(v_s)
        else:
            w_in = w_b_in[i]
            c1, c2 = B_Q_RANK + B_KV_RANK, B_Q_RANK + B_KV_RANK + B_ROPE
            w_kr = w_in[:, c1:c2]
            w1 = jnp.concatenate([w_in[:, :c1], w_kr, _swap_halves(w_kr)], axis=1).astype(BF16)
            wg = w_in[:, c2:].astype(BF16)
            wuq = w_b_uq[i].reshape(B_Q_RANK, B_HEADS, B_QK)
            wuq_rope = wuq[..., B_NOPE:]
            wuq = jnp.concatenate([wuq[..., :B_NOPE], wuq_rope, _swap_halves(wuq_rope)],
                                  axis=-1).reshape(B_Q_RANK, B_HEADS * B_CAT).astype(BF16)
            g_qr, g_kr = b_q_rope_norm[i], b_k_rope_norm[i]
            gq = _row(jnp.concatenate([b_q_nope_norm[i], g_qr, _swap_halves(g_qr)]))
            gkr = _row(jnp.concatenate([g_kr, _swap_halves(g_kr)]))
            params = (_row(b_ln[i]), w1, wg, wuq, _row(b_q_a_norm[i]), _row(b_kv_a_norm[i]), gq, gkr,
                      w_b_uk[i].astype(BF16), w_b_uv[i].astype(BF16), _row(b_k_nope_norm[i]),
                      w_b_out[i].astype(BF16))
            yp, c_p, r_p = _layer_b(yp, None, None, jnp.arange(seq), *params, batch=batch, seq=seq,
                                    tm=512, tm_out=512)
            ys, c_s, r_s = _layer_b(ys, cache_b_ckv[i], cache_b_krope[i], PAST_LEN + jnp.arange(dec_seq),
                                    *params, batch=dec_batch, seq=dec_seq, tm=256, tm_out=dec_tokens)
            bcp.append(c_p); brp.append(r_p); bcs.append(c_s); brs.append(r_s)
    yp = yp.reshape(batch, seq, D_MODEL)
    ys = ys.reshape(dec_batch, dec_seq, D_MODEL)
    return (yp, ys,
            jnp.stack(akp), jnp.stack(avp), jnp.stack(bcp), jnp.stack(brp),
            jnp.stack(aks), jnp.stack(avs), jnp.stack(bcs), jnp.stack(brs))
```

```python
import functools
import math

import numpy as np

import jax
import jax.numpy as jnp
from jax import lax
from jax.experimental import pallas as pl
from jax.experimental.pallas import tpu as pltpu

D_MODEL = 2048
CHUNK = 64
EPS = 1e-6
NEG_INF = -1e30
PAST_LEN = 1024
LOG2E = math.log2(math.e)

A_HEADS = 16
A_HEAD_DIM = 128
A_WIDTH = A_HEADS * A_HEAD_DIM
A_WINDOW = 8 * CHUNK
A_BAND = A_WINDOW + CHUNK
REL_CLIP = 128
A_SCALE = A_HEAD_DIM ** -0.5
A_PROJ_SEG = 256

B_HEADS = 16
B_NOPE = 128
B_ROPE = 64
B_V = 128
B_QK = B_NOPE + B_ROPE
B_Q_RANK = 512
B_KV_RANK = 512
B_WIDTH = B_HEADS * B_V
B_SCALE = B_QK ** -0.5
ROPE_THETA = 10000.0
B_CAT = B_NOPE + 2 * B_ROPE
B_LATENT_COLS = B_Q_RANK + B_KV_RANK + 2 * B_ROPE
BF16_SUBLANES = 16
VT_ROWS = B_V + BF16_SUBLANES

LANES = 128
VMEM_LIMIT = 56 * 1024 * 1024

A_PROJ_ROWS = 1024
A_Q_ROWS = 4 * CHUNK
OUT_ROWS = 512
B_PROJ_ROWS = 512
B_SAMPLE_PROJ_ROWS = 256
B_EXPAND_ROWS = 512
B_ATTN_ROWS = 512
B_KEY_BLOCKS = 2

F32 = jnp.float32
BF16 = jnp.bfloat16
_NT = (((1,), (1,)), ((), ()))


def _rms_scale(x):
    return lax.rsqrt(jnp.mean(x * x, axis=-1, keepdims=True) + EPS)


def _const_spec(shape):
    return pl.BlockSpec(shape, lambda *_: (0,) * len(shape), pipeline_mode=pl.Buffered(1))


def _proj_a_kernel(x_ref, ln_ref, wq_ref, wk_ref, wv_ref, wg_ref, gq_ref, gk_ref,
                   q_ref, k_ref, v_ref, sg_ref, k32_ref, v32_ref, h_ref, *, ts, tiles_per_keep, keep_rows):
    i = pl.program_id(0)
    j = pl.program_id(1)
    tm = x_ref.shape[0]

    @pl.when(j == 0)
    def _():
        x = x_ref[...]
        h_ref[...] = (x * _rms_scale(x) * ln_ref[...]).astype(BF16)

    h = h_ref[...]
    q = jnp.dot(h, wq_ref[...], preferred_element_type=F32)
    gq = gq_ref[...] * (A_SCALE * LOG2E)
    for hh in range(ts // LANES):
        sl = slice(hh * LANES, (hh + 1) * LANES)
        s = q[:, sl]
        q_ref[:, sl] = (s * _rms_scale(s) * gq).astype(BF16)

    k = jnp.dot(h, wk_ref[...], preferred_element_type=F32)
    gk = gk_ref[...]
    k_tails = []
    for hh in range(ts // LANES):
        sl = slice(hh * LANES, (hh + 1) * LANES)
        s = k[:, sl]
        kn = s * _rms_scale(s) * gk
        k_ref[:, sl] = kn.astype(BF16)
        k_tails.append(kn[tm - keep_rows:, :])

    v = jnp.dot(h, wv_ref[...], preferred_element_type=F32)
    v_ref[...] = v.astype(BF16)
    g = jnp.dot(h, wg_ref[...], preferred_element_type=F32)
    sg_ref[...] = (g / (1.0 + jnp.exp(-g))).astype(BF16)

    @pl.when((i % tiles_per_keep) == (tiles_per_keep - 1))
    def _():
        for hh, kn_tail in enumerate(k_tails):
            k32_ref[:, hh * LANES:(hh + 1) * LANES] = kn_tail
        v32_ref[...] = v[tm - keep_rows:, :]


def _proj_a(x, ln, w, gq, gk, *, tm, ts, keep_rows, tiles_per_keep):
    t = x.shape[0]
    nm = t // tm
    nkeep = nm // tiles_per_keep
    nj = A_WIDTH // ts

    def w_spec(part):
        return pl.BlockSpec((D_MODEL, ts), lambda i, j: (0, part * nj + j))

    def tail_map(i, j):
        is_tail = (i % tiles_per_keep) == (tiles_per_keep - 1)
        return (i // tiles_per_keep, jnp.where(is_tail, j, 0))

    kernel = functools.partial(_proj_a_kernel, ts=ts, tiles_per_keep=tiles_per_keep, keep_rows=keep_rows)
    act = jax.ShapeDtypeStruct((t, A_WIDTH), BF16)
    tail = jax.ShapeDtypeStruct((nkeep * keep_rows, A_WIDTH), F32)
    return pl.pallas_call(
        kernel,
        grid=(nm, nj),
        in_specs=[
            pl.BlockSpec((tm, D_MODEL), lambda i, j: (i, 0)),
            _const_spec((1, D_MODEL)),
            w_spec(0), w_spec(1), w_spec(2), w_spec(3),
            _const_spec((1, A_HEAD_DIM)),
            _const_spec((1, A_HEAD_DIM)),
        ],
        out_specs=[pl.BlockSpec((tm, ts), lambda i, j: (i, j))] * 4
                  + [pl.BlockSpec((keep_rows, ts), tail_map)] * 2,
        out_shape=[act] * 4 + [tail] * 2,
        scratch_shapes=[pltpu.VMEM((tm, D_MODEL), BF16)],
        compiler_params=pltpu.CompilerParams(
            dimension_semantics=("arbitrary", "arbitrary"), vmem_limit_bytes=VMEM_LIMIT),
        name="proj_a",
    )(x, ln, w, w, w, w, gq, gk)


def _attn_a_kernel(*refs, has_cache, tq, seq, n_bias, heads):
    if has_cache:
        q_ref, k_ref, v_ref, kc_ref, vc_ref, bias_ref, o_ref, kpad, vpad, s_ref = refs
    else:
        q_ref, k_ref, v_ref, bias_ref, o_ref, kpad, vpad, s_ref = refs
    width = A_WINDOW + tq
    n_blocks = seq // tq

    for hh in range(heads):
        cols = slice(hh * A_HEAD_DIM, (hh + 1) * A_HEAD_DIM)
        if has_cache:
            kpad[:, 0:A_WINDOW] = kc_ref[0, hh].T
            vpad[0:A_WINDOW, :] = vc_ref[0, hh]
        else:
            kpad[:, 0:A_WINDOW] = jnp.zeros((A_HEAD_DIM, A_WINDOW), BF16)
            vpad[0:A_WINDOW, :] = jnp.zeros((A_WINDOW, A_HEAD_DIM), BF16)
        kpad[:, A_WINDOW:A_WINDOW + seq] = k_ref[0, :, cols].T
        vpad[A_WINDOW:A_WINDOW + seq, :] = v_ref[0, :, cols]

        def scores(t):
            q = q_ref[0, t * tq:(t + 1) * tq, cols]
            kb_t = kpad[:, t * tq:t * tq + width]
            s_ref[t % 2] = jnp.dot(q, kb_t, preferred_element_type=F32)

        scores(0)
        for t in range(n_blocks):
            if t + 1 < n_blocks:
                scores(t + 1)
            s = s_ref[t % 2] + bias_ref[hh, min(t, n_bias - 1)]
            m = jnp.max(s, axis=-1, keepdims=True)
            p = jnp.exp2(s - m)
            l = jnp.sum(p, axis=-1, keepdims=True)
            vb = vpad[t * tq:t * tq + width, :]
            o = jnp.dot(p.astype(BF16), vb, preferred_element_type=F32) / l
            o_ref[0, t * tq:(t + 1) * tq, cols] = o.astype(BF16)


def _attn_a(q, k, v, bias, k_cache, v_cache, *, batch, seq, tq, heads):
    has_cache = k_cache is not None
    width = A_WINDOW + tq
    n_bias = bias.shape[1]
    in_specs = [pl.BlockSpec((1, seq, heads * A_HEAD_DIM), lambda b, h: (b, 0, h))] * 3
    args = [a.reshape(batch, seq, A_WIDTH) for a in (q, k, v)]
    if has_cache:
        in_specs += [pl.BlockSpec((1, heads, A_WINDOW, A_HEAD_DIM), lambda b, h: (b, h, 0, 0))] * 2
        args += [k_cache, v_cache]
    in_specs.append(pl.BlockSpec((heads, n_bias, tq, width), lambda b, h: (h, 0, 0, 0)))
    args.append(bias)
    kernel = functools.partial(_attn_a_kernel, has_cache=has_cache, tq=tq, seq=seq, n_bias=n_bias, heads=heads)
    out = pl.pallas_call(
        kernel,
        grid=(batch, A_HEADS // heads),
        in_specs=in_specs,
        out_specs=pl.BlockSpec((1, seq, heads * A_HEAD_DIM), lambda b, h: (b, 0, h)),
        out_shape=jax.ShapeDtypeStruct((batch, seq, A_WIDTH), BF16),
        scratch_shapes=[pltpu.VMEM((A_HEAD_DIM, A_WINDOW + seq), BF16),
                        pltpu.VMEM((A_WINDOW + seq, A_HEAD_DIM), BF16),
                        pltpu.VMEM((2, tq, width), F32)],
        compiler_params=pltpu.CompilerParams(
            dimension_semantics=("arbitrary", "arbitrary"), vmem_limit_bytes=VMEM_LIMIT),
        name="attn_a",
    )(*args)
    return out.reshape(batch * seq, A_WIDTH)


def _band_bias(table, tq, mask_prefix):
    width = A_WINDOW + tq
    length = CHUNK + A_BAND - 1
    n_table = 2 * REL_CLIP + 1
    c0 = A_WINDOW + REL_CLIP - (A_BAND - 1)
    n_lo, first = max(0, -c0), max(0, c0)
    n_hi = length - n_lo - (n_table - first)
    heads = table.shape[0]
    v = jnp.concatenate([jnp.broadcast_to(table[:, :1], (heads, n_lo)), table[:, first:],
                         jnp.broadcast_to(table[:, -1:], (heads, n_hi))], axis=1).astype(F32)
    w = jnp.pad(v[:, ::-1], ((0, 0), (0, 1)))
    skew = jnp.tile(w, (1, CHUNK))[:, :CHUNK * length].reshape(heads, CHUNK, length)
    band = skew[:, :, CHUNK - 1:CHUNK - 1 + A_BAND] * LOG2E
    chunks = [jnp.pad(band, ((0, 0), (0, 0), (c * CHUNK, tq - CHUNK - c * CHUNK)), constant_values=NEG_INF)
              for c in range(tq // CHUNK)]
    block = jnp.concatenate(chunks, axis=1)
    if not mask_prefix:
        return block[:, None]
    j = np.arange(width)[None, :]
    keep = np.stack([np.broadcast_to(j + v_idx * tq >= A_WINDOW, (tq, width))
                     for v_idx in range(A_WINDOW // tq + 1)])
    return jnp.where(keep[None], block[:, None], NEG_INF)


def _out_proj_kernel(x_ref, sg_ref, o_ref, w_ref, y_ref):
    gated = sg_ref[...] * o_ref[...]
    y_ref[...] = x_ref[...] + jnp.dot(gated, w_ref[...], preferred_element_type=F32)


def _out_proj(x, sg, o, w, *, tm):
    t, width = o.shape
    return pl.pallas_call(
        _out_proj_kernel,
        grid=(t // tm,),
        in_specs=[
            pl.BlockSpec((tm, D_MODEL), lambda i: (i, 0)),
            pl.BlockSpec((tm, width), lambda i: (i, 0)),
            pl.BlockSpec((tm, width), lambda i: (i, 0)),
            _const_spec((width, D_MODEL)),
        ],
        out_specs=pl.BlockSpec((tm, D_MODEL), lambda i: (i, 0)),
        out_shape=jax.ShapeDtypeStruct((t, D_MODEL), F32),
        compiler_params=pltpu.CompilerParams(
            dimension_semantics=("arbitrary",), vmem_limit_bytes=VMEM_LIMIT),
        name="out_proj",
    )(x, sg, o, w)


Q_HEADS_PER_DOT = 4
GATE_CHUNK = 512


def _proj_b_kernel(x_ref, ln_ref, w_ref, wuq_ref, gqa_ref, gkva_ref, gq_ref, gkr_ref, cs_ref, csq_ref,
                   q_ref, ckv_ref, kr2_ref, sg_ref, *, transposed_q):
    x = x_ref[...]
    h = (x * _rms_scale(x) * ln_ref[...]).astype(BF16)
    t1 = jnp.dot(h, w_ref[:, :B_LATENT_COLS], preferred_element_type=F32)
    cq = t1[:, :B_Q_RANK]
    ckv = t1[:, B_Q_RANK:B_Q_RANK + B_KV_RANK]
    kr = t1[:, B_Q_RANK + B_KV_RANK:]

    ckv_ref[...] = ckv * _rms_scale(ckv) * gkva_ref[...]
    wk = kr * _rms_scale(kr) * gkr_ref[...] * cs_ref[...]
    kr2_ref[...] = wk + pltpu.roll(wk, B_ROPE, axis=1)

    cqn = (cq * _rms_scale(cq) * gqa_ref[...]).astype(BF16)
    if transposed_q:
        g_nope = gq_ref[:B_NOPE, :] * (B_SCALE * LOG2E)
        g_rope = gq_ref[B_NOPE:, :] * csq_ref[...] * (B_SCALE * LOG2E)
        rows = Q_HEADS_PER_DOT * B_CAT
        for grp in range(B_HEADS // Q_HEADS_PER_DOT):
            qt = lax.dot_general(wuq_ref[grp * rows:(grp + 1) * rows, :], cqn, _NT, preferred_element_type=F32)
            for hh in range(Q_HEADS_PER_DOT):
                head = grp * Q_HEADS_PER_DOT + hh
                qn = qt[hh * B_CAT:hh * B_CAT + B_NOPE, :]
                qr = qt[hh * B_CAT + B_NOPE:(hh + 1) * B_CAT, :]
                sn = lax.rsqrt(jnp.mean(qn * qn, axis=0, keepdims=True) + EPS)
                sr = lax.rsqrt(jnp.mean(qr * qr, axis=0, keepdims=True) + EPS)
                q_ref[0, head, 0:B_NOPE, :] = (qn * sn * g_nope).astype(BF16)
                q_ref[0, head, B_NOPE:B_CAT, :] = (qr * sr * g_rope).astype(BF16)
    else:
        q = jnp.dot(cqn, wuq_ref[...], preferred_element_type=F32)
        g_nope = gq_ref[:, :B_NOPE] * (B_SCALE * LOG2E)
        g_rope = gq_ref[:, B_NOPE:] * csq_ref[...] * (B_SCALE * LOG2E)
        for hh in range(B_HEADS):
            qn = q[:, hh * B_CAT:hh * B_CAT + B_NOPE]
            qr = q[:, hh * B_CAT + B_NOPE:(hh + 1) * B_CAT]
            q_ref[:, hh * B_CAT:hh * B_CAT + B_NOPE] = (qn * _rms_scale(qn) * g_nope).astype(BF16)
            q_ref[:, hh * B_CAT + B_NOPE:(hh + 1) * B_CAT] = (qr * _rms_scale(qr) * g_rope).astype(BF16)

    for c in range(B_WIDTH // GATE_CHUNK):
        sl = slice(c * GATE_CHUNK, (c + 1) * GATE_CHUNK)
        g = jnp.dot(h, w_ref[:, B_LATENT_COLS + c * GATE_CHUNK:B_LATENT_COLS + (c + 1) * GATE_CHUNK],
                    preferred_element_type=F32)
        sg_ref[:, sl] = (g / (1.0 + jnp.exp(-g))).astype(BF16)


def _proj_b(x, ln, w, wuq, gqa, gkva, gq, gkr, cs, *, tm, batch=None):
    t = x.shape[0]
    n_pos_tiles = cs.shape[0] // tm
    transposed_q = batch is not None
    pos_spec = pl.BlockSpec((tm, 2 * B_ROPE), lambda i: (i % n_pos_tiles, 0))
    if transposed_q:
        tiles = t // batch // tm
        wuq = wuq.T
        gq = jnp.broadcast_to(gq.reshape(B_CAT, 1), (B_CAT, tm))
        csq = cs.T
        gq_spec = _const_spec((B_CAT, tm))
        csq_spec = pl.BlockSpec((2 * B_ROPE, tm), lambda i: (0, i % n_pos_tiles))
        q_spec = pl.BlockSpec((1, B_HEADS, B_CAT, tm), lambda i: (i // tiles, 0, 0, i % tiles))
        q_shape = jax.ShapeDtypeStruct((batch, B_HEADS, B_CAT, t // batch), BF16)
    else:
        csq = cs
        gq_spec = _const_spec((1, B_CAT))
        csq_spec = pos_spec
        q_spec = pl.BlockSpec((tm, B_HEADS * B_CAT), lambda i: (i, 0))
        q_shape = jax.ShapeDtypeStruct((t, B_HEADS * B_CAT), BF16)
    return pl.pallas_call(
        functools.partial(_proj_b_kernel, transposed_q=transposed_q),
        grid=(t // tm,),
        in_specs=[
            pl.BlockSpec((tm, D_MODEL), lambda i: (i, 0)),
            _const_spec((1, D_MODEL)),
            _const_spec(w.shape),
            _const_spec(wuq.shape),
            _const_spec((1, B_Q_RANK)),
            _const_spec((1, B_KV_RANK)),
            gq_spec,
            _const_spec((1, 2 * B_ROPE)),
            pos_spec,
            csq_spec,
        ],
        out_specs=[
            q_spec,
            pl.BlockSpec((tm, B_KV_RANK), lambda i: (i, 0)),
            pl.BlockSpec((tm, 2 * B_ROPE), lambda i: (i, 0)),
            pl.BlockSpec((tm, B_WIDTH), lambda i: (i, 0)),
        ],
        out_shape=[
            q_shape,
            jax.ShapeDtypeStruct((t, B_KV_RANK), F32),
            jax.ShapeDtypeStruct((t, 2 * B_ROPE), F32),
            jax.ShapeDtypeStruct((t, B_WIDTH), BF16),
        ],
        compiler_params=pltpu.CompilerParams(
            dimension_semantics=("arbitrary",), vmem_limit_bytes=VMEM_LIMIT),
        name="proj_b",
    )(x, ln, w, wuq, gqa, gkva, gq, gkr, cs, csq)


def _expand_b_kernel(ckv_ref, kr2_ref, wuk_ref, wuv_ref, gkn_ref, kcat_ref, v_ref, *, transposed_v):
    c = ckv_ref[...].astype(BF16)
    kn = jnp.dot(c, wuk_ref[...], preferred_element_type=F32)
    krb = kr2_ref[...].astype(BF16)
    g = gkn_ref[...]
    for hh in range(B_HEADS):
        s = kn[:, hh * B_NOPE:(hh + 1) * B_NOPE]
        kcat_ref[:, hh * B_CAT:hh * B_CAT + B_NOPE] = (s * _rms_scale(s) * g).astype(BF16)
        kcat_ref[:, hh * B_CAT + B_NOPE:(hh + 1) * B_CAT] = krb
    if transposed_v:
        vt = lax.dot_general(wuv_ref[...], c, _NT, preferred_element_type=F32).astype(BF16)
        tm = c.shape[0]
        for hh in range(B_HEADS):
            v_ref[0, hh, 0, 0:B_V, :] = vt[hh * B_V:(hh + 1) * B_V, :]
            v_ref[0, hh, 0, B_V:VT_ROWS, :] = jnp.ones((VT_ROWS - B_V, tm), BF16)
    else:
        v_ref[...] = jnp.dot(c, wuv_ref[...], preferred_element_type=F32).astype(BF16)


def _expand_b(ckv, kr2, wuk, wuv, gkn, *, tm, batch=None):
    t = ckv.shape[0]
    transposed_v = batch is not None
    if transposed_v:
        tiles = t // batch // tm
        v_spec = pl.BlockSpec((1, B_HEADS, 1, VT_ROWS, tm), lambda i: (i // tiles, 0, i % tiles, 0, 0))
        v_shape = jax.ShapeDtypeStruct((batch, B_HEADS, tiles, VT_ROWS, tm), BF16)
    else:
        v_spec = pl.BlockSpec((tm, B_WIDTH), lambda i: (i, 0))
        v_shape = jax.ShapeDtypeStruct((t, B_WIDTH), BF16)
    return pl.pallas_call(
        functools.partial(_expand_b_kernel, transposed_v=transposed_v),
        grid=(t // tm,),
        in_specs=[
            pl.BlockSpec((tm, B_KV_RANK), lambda i: (i, 0)),
            pl.BlockSpec((tm, 2 * B_ROPE), lambda i: (i, 0)),
            _const_spec(wuk.shape),
            _const_spec(wuv.shape),
            _const_spec((1, B_NOPE)),
        ],
        out_specs=[pl.BlockSpec((tm, B_HEADS * B_CAT), lambda i: (i, 0)), v_spec],
        out_shape=[jax.ShapeDtypeStruct((t, B_HEADS * B_CAT), BF16), v_shape],
        compiler_params=pltpu.CompilerParams(
            dimension_semantics=("arbitrary",), vmem_limit_bytes=VMEM_LIMIT),
        name="expand_b",
    )(ckv, kr2, wuk, wuv, gkn)


def _attn_bt_kernel(q_ref, k_ref, vt_ref, o_ref, s_ref, p_ref, alpha_ref, mask_ref, m_ref, acc_ref,
                    *, tq, n_blocks, kv_blocks, heads):
    pairs = [(qi, k0, min(kv_blocks, qi + 1 - k0))
             for qi in range(n_blocks) for k0 in range(0, qi + 1, kv_blocks)]

    key_chunk = lax.broadcasted_iota(jnp.int32, (tq, tq), 0) // CHUNK
    q_chunk = lax.broadcasted_iota(jnp.int32, (tq, tq), 1) // CHUNK
    mask_ref[...] = jnp.where(key_chunk <= q_chunk, 0.0, NEG_INF).astype(F32)

    half = tq // 2

    def stages(hh):
        s_h, p_h, alpha_h = s_ref.at[hh], p_ref.at[hh], alpha_ref.at[hh]
        m_h, acc_h = m_ref.at[hh], acc_ref.at[hh]

        def scores(t):
            qi, k0, nk = pairs[t]
            slot = t % 2
            q_t = q_ref[0, hh, :, qi * tq:(qi + 1) * tq]
            rows = nk * tq
            kb = k_ref[0, k0 * tq:k0 * tq + rows, hh * B_CAT:(hh + 1) * B_CAT]
            if k0 + nk - 1 == qi:
                full = rows - half
                s_h[slot, :full, :] = jnp.dot(kb[:full], q_t, preferred_element_type=F32)
                s_h[slot, full:rows, half:] = jnp.dot(kb[full:], q_t[:, half:], preferred_element_type=F32)
            else:
                s_h[slot, :rows, :] = jnp.dot(kb, q_t, preferred_element_type=F32)

        def regions(t):
            qi, k0, nk = pairs[t]
            rows = nk * tq
            if k0 + nk - 1 != qi:
                return [(0, rows, 0, tq, None)]
            base = rows - tq
            out = [(0, base, 0, tq, None)] if base else []
            return out + [(base, base + half, 0, half, 0), (base, base + half, half, tq, None),
                          (base + half, rows, half, tq, half)]

        def probabilities(t):
            qi, k0, nk = pairs[t]
            slot = t % 2
            parts = []
            for r0, r1, c0, c1, mrow in regions(t):
                s = s_h[slot, r0:r1, c0:c1]
                if mrow is not None:
                    s = s + mask_ref[mrow:mrow + (r1 - r0), c0:c1]
                parts.append(s)
            col_max = jnp.full((1, tq), NEG_INF, F32)
            for (r0, r1, c0, c1, _), s in zip(regions(t), parts):
                part_max = jnp.max(s, axis=0, keepdims=True)
                if c1 - c0 < tq:
                    pad = jnp.full((1, tq - (c1 - c0)), NEG_INF, F32)
                    part_max = jnp.concatenate([part_max, pad] if c0 == 0 else [pad, part_max], axis=1)
                col_max = jnp.maximum(col_max, part_max)
            if k0 == 0:
                m_new = col_max
            else:
                m_new = jnp.maximum(m_h[...], col_max)
                alpha_h[slot] = jnp.exp2(m_h[...] - m_new)
            m_h[...] = m_new
            for (r0, r1, c0, c1, _), s in zip(regions(t), parts):
                p_h[slot, r0:r1, c0:c1] = jnp.exp2(s - m_new[:, c0:c1]).astype(BF16)

        def values(t):
            qi, k0, nk = pairs[t]
            slot = t % 2
            diag = k0 + nk - 1 == qi

            def product(c0, c1, last_rows):
                total = None
                for b in range(nk):
                    n = last_rows if b == nk - 1 else tq
                    term = jnp.dot(vt_ref[0, hh, k0 + b, :, :n], p_h[slot, b * tq:b * tq + n, c0:c1],
                                   preferred_element_type=F32)
                    total = term if total is None else total + term
                return total

            if diag:
                pv = jnp.concatenate([product(0, half, half), product(half, tq, tq)], axis=1)
            else:
                pv = product(0, tq, tq)
            acc = pv if k0 == 0 else alpha_h[slot] * acc_h[...] + pv
            if diag:
                o_t = acc[:B_V, :] / acc[B_V:B_V + 1, :]
                o_ref[0, qi * tq:(qi + 1) * tq, hh * B_V:(hh + 1) * B_V] = o_t.T.astype(BF16)
            else:
                acc_h[...] = acc
        return scores, probabilities, values

    head_stages = [stages(hh) for hh in range(heads)]
    for scores, _, _ in head_stages:
        scores(0)
    for t in range(len(pairs) + 1):
        for scores, probabilities, values in head_stages:
            if t + 1 < len(pairs):
                scores(t + 1)
            if t < len(pairs):
                probabilities(t)
            if t >= 1:
                values(t - 1)


def _attn_bt(qt, kcat, vt, *, batch, seq, tq, kv_blocks, heads):
    k3 = kcat.reshape(batch, seq, B_HEADS * B_CAT)
    n_blocks = seq // tq
    out = pl.pallas_call(
        functools.partial(_attn_bt_kernel, tq=tq, n_blocks=n_blocks, kv_blocks=kv_blocks, heads=heads),
        grid=(batch, B_HEADS // heads),
        in_specs=[
            pl.BlockSpec((1, heads, B_CAT, seq), lambda b, h: (b, h, 0, 0)),
            pl.BlockSpec((1, seq, heads * B_CAT), lambda b, h: (b, 0, h)),
            pl.BlockSpec((1, heads, n_blocks, VT_ROWS, tq), lambda b, h: (b, h, 0, 0, 0)),
        ],
        out_specs=pl.BlockSpec((1, seq, heads * B_V), lambda b, h: (b, 0, h)),
        out_shape=jax.ShapeDtypeStruct((batch, seq, B_WIDTH), BF16),
        scratch_shapes=[pltpu.VMEM((heads, 2, kv_blocks * tq, tq), F32),
                        pltpu.VMEM((heads, 2, kv_blocks * tq, tq), BF16),
                        pltpu.VMEM((heads, 2, 1, tq), F32), pltpu.VMEM((tq, tq), F32),
                        pltpu.VMEM((heads, 1, tq), F32), pltpu.VMEM((heads, VT_ROWS, tq), F32)],
        compiler_params=pltpu.CompilerParams(
            dimension_semantics=("arbitrary", "arbitrary"), vmem_limit_bytes=VMEM_LIMIT),
        name="attn_bt",
    )(qt, k3, vt)
    return out.reshape(batch * seq, B_WIDTH)


def _attn_b_kernel(q_ref, k_ref, v_ref, o_ref):
    for hh in range(B_HEADS):
        q = q_ref[0, :, hh * B_CAT:(hh + 1) * B_CAT]
        kb = k_ref[0, :, hh * B_CAT:(hh + 1) * B_CAT]
        s = lax.dot_general(q, kb, _NT, preferred_element_type=F32)
        p = jnp.exp2(s - jnp.max(s, axis=-1, keepdims=True))
        l = jnp.sum(p, axis=-1, keepdims=True)
        vb = v_ref[0, :, hh * B_V:(hh + 1) * B_V]
        o = jnp.dot(p.astype(BF16), vb, preferred_element_type=F32) / l
        o_ref[0, :, hh * B_V:(hh + 1) * B_V] = o.astype(BF16)


def _attn_b(qcat, kcat, v, *, batch, q_len, k_len):
    q3 = qcat.reshape(batch, q_len, B_HEADS * B_CAT)
    k3 = kcat.reshape(batch, k_len, B_HEADS * B_CAT)
    v3 = v.reshape(batch, k_len, B_WIDTH)
    out = pl.pallas_call(
        _attn_b_kernel,
        grid=(batch,),
        in_specs=[
            pl.BlockSpec((1, q_len, B_HEADS * B_CAT), lambda b: (b, 0, 0)),
            pl.BlockSpec((1, k_len, B_HEADS * B_CAT), lambda b: (b, 0, 0)),
            pl.BlockSpec((1, k_len, B_WIDTH), lambda b: (b, 0, 0)),
        ],
        out_specs=pl.BlockSpec((1, q_len, B_WIDTH), lambda b: (b, 0, 0)),
        out_shape=jax.ShapeDtypeStruct((batch, q_len, B_WIDTH), BF16),
        compiler_params=pltpu.CompilerParams(
            dimension_semantics=("arbitrary",), vmem_limit_bytes=VMEM_LIMIT),
        name="attn_b",
    )(q3, k3, v3)
    return out.reshape(batch * q_len, B_WIDTH)


def _swap_halves(a):
    half = a.shape[-1] // 2
    return jnp.concatenate([a[..., half:], a[..., :half]], axis=-1)


def _rope_table(pos):
    half = B_ROPE // 2
    inv = ROPE_THETA ** (-jnp.arange(half, dtype=F32) / half)
    ang = pos.astype(F32)[:, None] * inv[None, :]
    cos, sin = jnp.cos(ang), jnp.sin(ang)
    return jnp.concatenate([cos, cos, -sin, sin], axis=-1)


def _row(v):
    return v.reshape(1, -1).astype(F32)


def _layer_a(x, k_cache, v_cache, ln, w_in, gq, gk, table, w_out, *, batch, seq, tm, tq, tm_out):
    if k_cache is None:
        keep = min(A_WINDOW, seq)
        keep_rows, tiles_per_keep = keep, seq // tm
    else:
        keep = seq
        keep_rows, tiles_per_keep = tm, 1
    q, k, v, sg, k32, v32 = _proj_a(x, ln, w_in, gq, gk, tm=tm, ts=A_PROJ_SEG,
                                    keep_rows=keep_rows, tiles_per_keep=tiles_per_keep)
    bias = _band_bias(table, tq, mask_prefix=k_cache is None)
    o = _attn_a(q, k, v, bias, k_cache, v_cache, batch=batch, seq=seq, tq=tq,
                heads=1 if k_cache is None else A_HEADS)
    y = _out_proj(x, sg, o, w_out, tm=tm_out)
    shape = (batch, keep, A_HEADS, A_HEAD_DIM)
    return y, k32.reshape(shape), v32.reshape(shape)


def _layer_b(x, ckv_cache, kr_cache, pos, ln, w, wuq, gqa, gkva, gq, gkr, wuk, wuv, gkn, w_out,
             *, batch, seq, tm, tm_out):
    cs = _rope_table(pos)
    if ckv_cache is not None:
        cs = jnp.tile(cs, (batch, 1))
    qcat, ckv, kr2, sg = _proj_b(x, ln, w, wuq, gqa, gkva, gq, gkr, cs, tm=tm,
                                 batch=batch if ckv_cache is None else None)
    if ckv_cache is None:
        kcat, vt = _expand_b(ckv, kr2, wuk, wuv.T, gkn, tm=B_ATTN_ROWS, batch=batch)
        o = _attn_bt(qcat, kcat, vt, batch=batch, seq=seq, tq=B_ATTN_ROWS, kv_blocks=B_KEY_BLOCKS, heads=1)
    else:
        k_len = ckv_cache.shape[1] + seq
        ckv_all = jnp.concatenate(
            [ckv_cache, ckv.reshape(batch, seq, B_KV_RANK)], axis=1).reshape(batch * k_len, B_KV_RANK)
        kr2_all = jnp.concatenate(
            [jnp.concatenate([kr_cache, kr_cache], axis=-1), kr2.reshape(batch, seq, 2 * B_ROPE)],
            axis=1).reshape(batch * k_len, 2 * B_ROPE)
        kcat, v = _expand_b(ckv_all, kr2_all, wuk, wuv, gkn, tm=B_EXPAND_ROWS)
        o = _attn_b(qcat, kcat, v, batch=batch, q_len=seq, k_len=k_len)
    y = _out_proj(x, sg, o, w_out, tm=tm_out)
    return y, ckv.reshape(batch, seq, B_KV_RANK), kr2[:, :B_ROPE].reshape(batch, seq, B_ROPE)


def kernel(x_prompt, x_sample, cache_a_k, cache_a_v, cache_b_ckv, cache_b_krope, a_ln, w_a_in, a_q_norm, a_k_norm, a_rel_bias, w_a_out, b_ln, w_b_in, b_q_a_norm, w_b_uq, b_kv_a_norm, w_b_uk, w_b_uv, b_q_nope_norm, b_k_nope_norm, b_q_rope_norm, b_k_rope_norm, w_b_out):
    batch, seq, _ = x_prompt.shape
    dec_batch, dec_seq, _ = x_sample.shape
    dec_tokens = dec_batch * dec_seq
    depth = a_ln.shape[0] + b_ln.shape[0]
    yp = x_prompt.reshape(batch * seq, D_MODEL)
    ys = x_sample.reshape(dec_tokens, D_MODEL)
    akp, avp, aks, avs, bcp, brp, bcs, brs = ([] for _ in range(8))
    for layer in range(depth):
        i = layer // 2
        if layer % 2 == 0:
            params = (_row(a_ln[i]), w_a_in[i].astype(BF16), _row(a_q_norm[i]), _row(a_k_norm[i]),
                      a_rel_bias[i], w_a_out[i].astype(BF16))
            yp, k_p, v_p = _layer_a(yp, None, None, *params, batch=batch, seq=seq,
                                    tm=A_PROJ_ROWS, tq=A_Q_ROWS, tm_out=OUT_ROWS)
            kc = cache_a_k[i].transpose(0, 2, 1, 3).astype(BF16)
            vc = cache_a_v[i].transpose(0, 2, 1, 3).astype(BF16)
            ys, k_s, v_s = _layer_a(ys, kc, vc, *params, batch=dec_batch, seq=dec_seq,
                                    tm=dec_tokens, tq=CHUNK, tm_out=dec_tokens)
            akp.append(k_p); avp.append(v_p); aks.append(k_s); avs.append(v_s)
        else:
            w_in = w_b_in[i]
            c1, c2 = B_Q_RANK + B_KV_RANK, B_Q_RANK + B_KV_RANK + B_ROPE
            w = jnp.concatenate([w_in[:, :c2], _swap_halves(w_in[:, c1:c2]), w_in[:, c2:]], axis=1).astype(BF16)
            wuq = w_b_uq[i].reshape(B_Q_RANK, B_HEADS, B_QK)
            wuq_rope = wuq[..., B_NOPE:]
            wuq = jnp.concatenate([wuq[..., :B_NOPE], wuq_rope, _swap_halves(wuq_rope)],
                                  axis=-1).reshape(B_Q_RANK, B_HEADS * B_CAT).astype(BF16)
            g_qr, g_kr = b_q_rope_norm[i], b_k_rope_norm[i]
            gq = _row(jnp.concatenate([b_q_nope_norm[i], g_qr, _swap_halves(g_qr)]))
            gkr = _row(jnp.concatenate([g_kr, _swap_halves(g_kr)]))
            params = (_row(b_ln[i]), w, wuq, _row(b_q_a_norm[i]), _row(b_kv_a_norm[i]), gq, gkr,
                      w_b_uk[i].astype(BF16), w_b_uv[i].astype(BF16), _row(b_k_nope_norm[i]),
                      w_b_out[i].astype(BF16))
            yp, c_p, r_p = _layer_b(yp, None, None, jnp.arange(seq), *params, batch=batch, seq=seq,
                                    tm=B_PROJ_ROWS, tm_out=OUT_ROWS)
            ys, c_s, r_s = _layer_b(ys, cache_b_ckv[i], cache_b_krope[i], PAST_LEN + jnp.arange(dec_seq),
                                    *params, batch=dec_batch, seq=dec_seq, tm=B_SAMPLE_PROJ_ROWS,
                                    tm_out=dec_tokens)
            bcp.append(c_p); brp.append(r_p); bcs.append(c_s); brs.append(r_s)
    yp = yp.reshape(batch, seq, D_MODEL)
    ys = ys.reshape(dec_batch, dec_seq, D_MODEL)
    return (yp, ys,
            jnp.stack(akp), jnp.stack(avp), jnp.stack(bcp), jnp.stack(brp),
            jnp.stack(aks), jnp.stack(avs), jnp.stack(bcs), jnp.stack(brs))
```

```python
import functools
import math

import numpy as np

import jax
import jax.numpy as jnp
from jax import lax
from jax.experimental import pallas as pl
from jax.experimental.pallas import tpu as pltpu

D_MODEL = 2048
CHUNK = 64
EPS = 1e-6
NEG_INF = -1e30
PAST_LEN = 1024
LOG2E = math.log2(math.e)

A_HEADS = 16
A_HEAD_DIM = 128
A_WIDTH = A_HEADS * A_HEAD_DIM
A_WINDOW = 8 * CHUNK
A_BAND = A_WINDOW + CHUNK
REL_CLIP = 128
A_SCALE = A_HEAD_DIM ** -0.5
A_PROJ_SEG = 256

B_HEADS = 16
B_NOPE = 128
B_ROPE = 64
B_V = 128
B_QK = B_NOPE + B_ROPE
B_Q_RANK = 512
B_KV_RANK = 512
B_WIDTH = B_HEADS * B_V
B_SCALE = B_QK ** -0.5
ROPE_THETA = 10000.0
B_CAT = B_NOPE + 2 * B_ROPE
B_LATENT_COLS = B_Q_RANK + B_KV_RANK + 2 * B_ROPE
BF16_SUBLANES = 16
VT_ROWS = B_V + BF16_SUBLANES

LANES = 128
VMEM_LIMIT = 56 * 1024 * 1024

A_PROJ_ROWS = 1024
A_Q_ROWS = 4 * CHUNK
OUT_ROWS = 512
B_PROJ_ROWS = 512
B_FUSED_PROJ_ROWS = 256
B_SAMPLE_PROJ_ROWS = 256
B_EXPAND_ROWS = 512
B_ATTN_ROWS = 512
B_KEY_BLOCKS = 2

F32 = jnp.float32
BF16 = jnp.bfloat16
_NT = (((1,), (1,)), ((), ()))


def _rms_scale(x):
    return lax.rsqrt(jnp.mean(x * x, axis=-1, keepdims=True) + EPS)


def _const_spec(shape):
    return pl.BlockSpec(shape, lambda *_: (0,) * len(shape), pipeline_mode=pl.Buffered(1))


def _proj_a_kernel(x_ref, ln_ref, wq_ref, wk_ref, wv_ref, wg_ref, gq_ref, gk_ref,
                   q_ref, k_ref, v_ref, sg_ref, k32_ref, v32_ref, h_ref, *, ts, tiles_per_keep, keep_rows):
    i = pl.program_id(0)
    j = pl.program_id(1)
    tm = x_ref.shape[0]

    @pl.when(j == 0)
    def _():
        x = x_ref[...]
        h_ref[...] = (x * _rms_scale(x) * ln_ref[...]).astype(BF16)

    h = h_ref[...]
    q = jnp.dot(h, wq_ref[...], preferred_element_type=F32)
    gq = gq_ref[...] * (A_SCALE * LOG2E)
    for hh in range(ts // LANES):
        sl = slice(hh * LANES, (hh + 1) * LANES)
        s = q[:, sl]
        q_ref[:, sl] = (s * _rms_scale(s) * gq).astype(BF16)

    k = jnp.dot(h, wk_ref[...], preferred_element_type=F32)
    gk = gk_ref[...]
    k_tails = []
    for hh in range(ts // LANES):
        sl = slice(hh * LANES, (hh + 1) * LANES)
        s = k[:, sl]
        kn = s * _rms_scale(s) * gk
        k_ref[:, sl] = kn.astype(BF16)
        k_tails.append(kn[tm - keep_rows:, :])

    v = jnp.dot(h, wv_ref[...], preferred_element_type=F32)
    v_ref[...] = v.astype(BF16)
    g = jnp.dot(h, wg_ref[...], preferred_element_type=F32)
    sg_ref[...] = (g / (1.0 + jnp.exp(-g))).astype(BF16)

    @pl.when((i % tiles_per_keep) == (tiles_per_keep - 1))
    def _():
        for hh, kn_tail in enumerate(k_tails):
            k32_ref[:, hh * LANES:(hh + 1) * LANES] = kn_tail
        v32_ref[...] = v[tm - keep_rows:, :]


def _proj_a(x, ln, w, gq, gk, *, tm, ts, keep_rows, tiles_per_keep):
    t = x.shape[0]
    nm = t // tm
    nkeep = nm // tiles_per_keep
    nj = A_WIDTH // ts

    def w_spec(part):
        return pl.BlockSpec((D_MODEL, ts), lambda i, j: (0, part * nj + j))

    def tail_map(i, j):
        is_tail = (i % tiles_per_keep) == (tiles_per_keep - 1)
        return (i // tiles_per_keep, jnp.where(is_tail, j, 0))

    kernel = functools.partial(_proj_a_kernel, ts=ts, tiles_per_keep=tiles_per_keep, keep_rows=keep_rows)
    act = jax.ShapeDtypeStruct((t, A_WIDTH), BF16)
    tail = jax.ShapeDtypeStruct((nkeep * keep_rows, A_WIDTH), F32)
    return pl.pallas_call(
        kernel,
        grid=(nm, nj),
        in_specs=[
            pl.BlockSpec((tm, D_MODEL), lambda i, j: (i, 0)),
            _const_spec((1, D_MODEL)),
            w_spec(0), w_spec(1), w_spec(2), w_spec(3),
            _const_spec((1, A_HEAD_DIM)),
            _const_spec((1, A_HEAD_DIM)),
        ],
        out_specs=[pl.BlockSpec((tm, ts), lambda i, j: (i, j))] * 4
                  + [pl.BlockSpec((keep_rows, ts), tail_map)] * 2,
        out_shape=[act] * 4 + [tail] * 2,
        scratch_shapes=[pltpu.VMEM((tm, D_MODEL), BF16)],
        compiler_params=pltpu.CompilerParams(
            dimension_semantics=("arbitrary", "arbitrary"), vmem_limit_bytes=VMEM_LIMIT),
        name="proj_a",
    )(x, ln, w, w, w, w, gq, gk)


def _attn_a_kernel(*refs, has_cache, tq, seq, n_bias, heads):
    if has_cache:
        q_ref, k_ref, v_ref, kc_ref, vc_ref, bias_ref, o_ref, kpad, vpad, s_ref = refs
    else:
        q_ref, k_ref, v_ref, bias_ref, o_ref, kpad, vpad, s_ref = refs
    width = A_WINDOW + tq
    n_blocks = seq // tq

    for hh in range(heads):
        cols = slice(hh * A_HEAD_DIM, (hh + 1) * A_HEAD_DIM)
        if has_cache:
            kpad[:, 0:A_WINDOW] = kc_ref[0, hh].T
            vpad[0:A_WINDOW, :] = vc_ref[0, hh]
        else:
            kpad[:, 0:A_WINDOW] = jnp.zeros((A_HEAD_DIM, A_WINDOW), BF16)
            vpad[0:A_WINDOW, :] = jnp.zeros((A_WINDOW, A_HEAD_DIM), BF16)
        kpad[:, A_WINDOW:A_WINDOW + seq] = k_ref[0, :, cols].T
        vpad[A_WINDOW:A_WINDOW + seq, :] = v_ref[0, :, cols]

        def scores(t):
            q = q_ref[0, t * tq:(t + 1) * tq, cols]
            kb_t = kpad[:, t * tq:t * tq + width]
            s_ref[t % 2] = jnp.dot(q, kb_t, preferred_element_type=F32)

        scores(0)
        for t in range(n_blocks):
            if t + 1 < n_blocks:
                scores(t + 1)
            s = s_ref[t % 2] + bias_ref[hh, min(t, n_bias - 1)]
            m = jnp.max(s, axis=-1, keepdims=True)
            p = jnp.exp2(s - m)
            l = jnp.sum(p, axis=-1, keepdims=True)
            vb = vpad[t * tq:t * tq + width, :]
            o = jnp.dot(p.astype(BF16), vb, preferred_element_type=F32) / l
            o_ref[0, t * tq:(t + 1) * tq, cols] = o.astype(BF16)


def _attn_a(q, k, v, bias, k_cache, v_cache, *, batch, seq, tq, heads):
    has_cache = k_cache is not None
    width = A_WINDOW + tq
    n_bias = bias.shape[1]
    in_specs = [pl.BlockSpec((1, seq, heads * A_HEAD_DIM), lambda b, h: (b, 0, h))] * 3
    args = [a.reshape(batch, seq, A_WIDTH) for a in (q, k, v)]
    if has_cache:
        in_specs += [pl.BlockSpec((1, heads, A_WINDOW, A_HEAD_DIM), lambda b, h: (b, h, 0, 0))] * 2
        args += [k_cache, v_cache]
    in_specs.append(pl.BlockSpec((heads, n_bias, tq, width), lambda b, h: (h, 0, 0, 0)))
    args.append(bias)
    kernel = functools.partial(_attn_a_kernel, has_cache=has_cache, tq=tq, seq=seq, n_bias=n_bias, heads=heads)
    out = pl.pallas_call(
        kernel,
        grid=(batch, A_HEADS // heads),
        in_specs=in_specs,
        out_specs=pl.BlockSpec((1, seq, heads * A_HEAD_DIM), lambda b, h: (b, 0, h)),
        out_shape=jax.ShapeDtypeStruct((batch, seq, A_WIDTH), BF16),
        scratch_shapes=[pltpu.VMEM((A_HEAD_DIM, A_WINDOW + seq), BF16),
                        pltpu.VMEM((A_WINDOW + seq, A_HEAD_DIM), BF16),
                        pltpu.VMEM((2, tq, width), F32)],
        compiler_params=pltpu.CompilerParams(
            dimension_semantics=("arbitrary", "arbitrary"), vmem_limit_bytes=VMEM_LIMIT),
        name="attn_a",
    )(*args)
    return out.reshape(batch * seq, A_WIDTH)


def _band_bias(table, tq, mask_prefix):
    width = A_WINDOW + tq
    length = CHUNK + A_BAND - 1
    n_table = 2 * REL_CLIP + 1
    c0 = A_WINDOW + REL_CLIP - (A_BAND - 1)
    n_lo, first = max(0, -c0), max(0, c0)
    n_hi = length - n_lo - (n_table - first)
    heads = table.shape[0]
    v = jnp.concatenate([jnp.broadcast_to(table[:, :1], (heads, n_lo)), table[:, first:],
                         jnp.broadcast_to(table[:, -1:], (heads, n_hi))], axis=1).astype(F32)
    w = jnp.pad(v[:, ::-1], ((0, 0), (0, 1)))
    skew = jnp.tile(w, (1, CHUNK))[:, :CHUNK * length].reshape(heads, CHUNK, length)
    band = skew[:, :, CHUNK - 1:CHUNK - 1 + A_BAND] * LOG2E
    chunks = [jnp.pad(band, ((0, 0), (0, 0), (c * CHUNK, tq - CHUNK - c * CHUNK)), constant_values=NEG_INF)
              for c in range(tq // CHUNK)]
    block = jnp.concatenate(chunks, axis=1)
    if not mask_prefix:
        return block[:, None]
    j = np.arange(width)[None, :]
    keep = np.stack([np.broadcast_to(j + v_idx * tq >= A_WINDOW, (tq, width))
                     for v_idx in range(A_WINDOW // tq + 1)])
    return jnp.where(keep[None], block[:, None], NEG_INF)


def _out_proj_kernel(x_ref, sg_ref, o_ref, w_ref, y_ref):
    gated = sg_ref[...] * o_ref[...]
    y_ref[...] = x_ref[...] + jnp.dot(gated, w_ref[...], preferred_element_type=F32)


def _out_proj(x, sg, o, w, *, tm):
    t, width = o.shape
    return pl.pallas_call(
        _out_proj_kernel,
        grid=(t // tm,),
        in_specs=[
            pl.BlockSpec((tm, D_MODEL), lambda i: (i, 0)),
            pl.BlockSpec((tm, width), lambda i: (i, 0)),
            pl.BlockSpec((tm, width), lambda i: (i, 0)),
            _const_spec((width, D_MODEL)),
        ],
        out_specs=pl.BlockSpec((tm, D_MODEL), lambda i: (i, 0)),
        out_shape=jax.ShapeDtypeStruct((t, D_MODEL), F32),
        compiler_params=pltpu.CompilerParams(
            dimension_semantics=("arbitrary",), vmem_limit_bytes=VMEM_LIMIT),
        name="out_proj",
    )(x, sg, o, w)


Q_HEADS_PER_DOT = 4
GATE_CHUNK = 512


def _proj_b_kernel(*refs, transposed_q, fused_residual):
    if fused_residual:
        (x_ref, sgp_ref, op_ref, wout_ref, ln_ref, w_ref, wuq_ref, gqa_ref, gkva_ref, gq_ref, gkr_ref, cs_ref,
         csq_ref, y_ref, q_ref, ckv_ref, kr2_ref, sg_ref) = refs
        x = x_ref[...] + jnp.dot(sgp_ref[...] * op_ref[...], wout_ref[...], preferred_element_type=F32)
        y_ref[...] = x
    else:
        (x_ref, ln_ref, w_ref, wuq_ref, gqa_ref, gkva_ref, gq_ref, gkr_ref, cs_ref, csq_ref,
         q_ref, ckv_ref, kr2_ref, sg_ref) = refs
        x = x_ref[...]
    h = (x * _rms_scale(x) * ln_ref[...]).astype(BF16)
    t1 = jnp.dot(h, w_ref[:, :B_LATENT_COLS], preferred_element_type=F32)
    cq = t1[:, :B_Q_RANK]
    ckv = t1[:, B_Q_RANK:B_Q_RANK + B_KV_RANK]
    kr = t1[:, B_Q_RANK + B_KV_RANK:]

    ckv_ref[...] = ckv * _rms_scale(ckv) * gkva_ref[...]
    wk = kr * _rms_scale(kr) * gkr_ref[...] * cs_ref[...]
    kr2_ref[...] = wk + pltpu.roll(wk, B_ROPE, axis=1)

    cqn = (cq * _rms_scale(cq) * gqa_ref[...]).astype(BF16)
    if transposed_q:
        g_nope = gq_ref[:B_NOPE, :] * (B_SCALE * LOG2E)
        g_rope = gq_ref[B_NOPE:, :] * csq_ref[...] * (B_SCALE * LOG2E)
        rows = Q_HEADS_PER_DOT * B_CAT
        for grp in range(B_HEADS // Q_HEADS_PER_DOT):
            qt = lax.dot_general(wuq_ref[grp * rows:(grp + 1) * rows, :], cqn, _NT, preferred_element_type=F32)
            for hh in range(Q_HEADS_PER_DOT):
                head = grp * Q_HEADS_PER_DOT + hh
                qn = qt[hh * B_CAT:hh * B_CAT + B_NOPE, :]
                qr = qt[hh * B_CAT + B_NOPE:(hh + 1) * B_CAT, :]
                sn = lax.rsqrt(jnp.mean(qn * qn, axis=0, keepdims=True) + EPS)
                sr = lax.rsqrt(jnp.mean(qr * qr, axis=0, keepdims=True) + EPS)
                q_ref[0, head, 0:B_NOPE, :] = (qn * sn * g_nope).astype(BF16)
                q_ref[0, head, B_NOPE:B_CAT, :] = (qr * sr * g_rope).astype(BF16)
    else:
        q = jnp.dot(cqn, wuq_ref[...], preferred_element_type=F32)
        g_nope = gq_ref[:, :B_NOPE] * (B_SCALE * LOG2E)
        g_rope = gq_ref[:, B_NOPE:] * csq_ref[...] * (B_SCALE * LOG2E)
        for hh in range(B_HEADS):
            qn = q[:, hh * B_CAT:hh * B_CAT + B_NOPE]
            qr = q[:, hh * B_CAT + B_NOPE:(hh + 1) * B_CAT]
            q_ref[:, hh * B_CAT:hh * B_CAT + B_NOPE] = (qn * _rms_scale(qn) * g_nope).astype(BF16)
            q_ref[:, hh * B_CAT + B_NOPE:(hh + 1) * B_CAT] = (qr * _rms_scale(qr) * g_rope).astype(BF16)

    for c in range(B_WIDTH // GATE_CHUNK):
        sl = slice(c * GATE_CHUNK, (c + 1) * GATE_CHUNK)
        g = jnp.dot(h, w_ref[:, B_LATENT_COLS + c * GATE_CHUNK:B_LATENT_COLS + (c + 1) * GATE_CHUNK],
                    preferred_element_type=F32)
        sg_ref[:, sl] = (g / (1.0 + jnp.exp(-g))).astype(BF16)


def _proj_b(x, ln, w, wuq, gqa, gkva, gq, gkr, cs, *, tm, batch=None, pending=None):
    t = x.shape[0]
    n_pos_tiles = cs.shape[0] // tm
    transposed_q = batch is not None
    pos_spec = pl.BlockSpec((tm, 2 * B_ROPE), lambda i: (i % n_pos_tiles, 0))
    if transposed_q:
        tiles = t // batch // tm
        wuq = wuq.T
        gq = jnp.broadcast_to(gq.reshape(B_CAT, 1), (B_CAT, tm))
        csq = cs.T
        gq_spec = _const_spec((B_CAT, tm))
        csq_spec = pl.BlockSpec((2 * B_ROPE, tm), lambda i: (0, i % n_pos_tiles))
        q_spec = pl.BlockSpec((1, B_HEADS, B_CAT, tm), lambda i: (i // tiles, 0, 0, i % tiles))
        q_shape = jax.ShapeDtypeStruct((batch, B_HEADS, B_CAT, t // batch), BF16)
    else:
        csq = cs
        gq_spec = _const_spec((1, B_CAT))
        csq_spec = pos_spec
        q_spec = pl.BlockSpec((tm, B_HEADS * B_CAT), lambda i: (i, 0))
        q_shape = jax.ShapeDtypeStruct((t, B_HEADS * B_CAT), BF16)
    row_spec = lambda width: pl.BlockSpec((tm, width), lambda i: (i, 0))
    in_specs = [row_spec(D_MODEL)]
    args = [x]
    out_specs, out_shape = [], []
    if pending is not None:
        sg_prev, o_prev, w_out = pending
        in_specs += [row_spec(sg_prev.shape[1]), row_spec(o_prev.shape[1]), _const_spec(w_out.shape)]
        args += [sg_prev, o_prev, w_out]
        out_specs.append(row_spec(D_MODEL))
        out_shape.append(jax.ShapeDtypeStruct((t, D_MODEL), F32))
    in_specs += [
        _const_spec((1, D_MODEL)),
        _const_spec(w.shape),
        _const_spec(wuq.shape),
        _const_spec((1, B_Q_RANK)),
        _const_spec((1, B_KV_RANK)),
        gq_spec,
        _const_spec((1, 2 * B_ROPE)),
        pos_spec,
        csq_spec,
    ]
    args += [ln, w, wuq, gqa, gkva, gq, gkr, cs, csq]
    out_specs += [q_spec, row_spec(B_KV_RANK), row_spec(2 * B_ROPE), row_spec(B_WIDTH)]
    out_shape += [
        q_shape,
        jax.ShapeDtypeStruct((t, B_KV_RANK), F32),
        jax.ShapeDtypeStruct((t, 2 * B_ROPE), F32),
        jax.ShapeDtypeStruct((t, B_WIDTH), BF16),
    ]
    return pl.pallas_call(
        functools.partial(_proj_b_kernel, transposed_q=transposed_q, fused_residual=pending is not None),
        grid=(t // tm,),
        in_specs=in_specs,
        out_specs=out_specs,
        out_shape=out_shape,
        compiler_params=pltpu.CompilerParams(
            dimension_semantics=("arbitrary",), vmem_limit_bytes=VMEM_LIMIT),
        name="proj_b",
    )(*args)


def _expand_b_kernel(ckv_ref, kr2_ref, wuk_ref, wuv_ref, gkn_ref, kcat_ref, v_ref, *, transposed_v):
    c = ckv_ref[...].astype(BF16)
    kn = jnp.dot(c, wuk_ref[...], preferred_element_type=F32)
    krb = kr2_ref[...].astype(BF16)
    g = gkn_ref[...]
    for hh in range(B_HEADS):
        s = kn[:, hh * B_NOPE:(hh + 1) * B_NOPE]
        kcat_ref[:, hh * B_CAT:hh * B_CAT + B_NOPE] = (s * _rms_scale(s) * g).astype(BF16)
        kcat_ref[:, hh * B_CAT + B_NOPE:(hh + 1) * B_CAT] = krb
    if transposed_v:
        vt = lax.dot_general(wuv_ref[...], c, _NT, preferred_element_type=F32).astype(BF16)
        tm = c.shape[0]
        for hh in range(B_HEADS):
            v_ref[0, hh, 0, 0:B_V, :] = vt[hh * B_V:(hh + 1) * B_V, :]
            v_ref[0, hh, 0, B_V:VT_ROWS, :] = jnp.ones((VT_ROWS - B_V, tm), BF16)
    else:
        v_ref[...] = jnp.dot(c, wuv_ref[...], preferred_element_type=F32).astype(BF16)


def _expand_b(ckv, kr2, wuk, wuv, gkn, *, tm, batch=None):
    t = ckv.shape[0]
    transposed_v = batch is not None
    if transposed_v:
        tiles = t // batch // tm
        v_spec = pl.BlockSpec((1, B_HEADS, 1, VT_ROWS, tm), lambda i: (i // tiles, 0, i % tiles, 0, 0))
        v_shape = jax.ShapeDtypeStruct((batch, B_HEADS, tiles, VT_ROWS, tm), BF16)
    else:
        v_spec = pl.BlockSpec((tm, B_WIDTH), lambda i: (i, 0))
        v_shape = jax.ShapeDtypeStruct((t, B_WIDTH), BF16)
    return pl.pallas_call(
        functools.partial(_expand_b_kernel, transposed_v=transposed_v),
        grid=(t // tm,),
        in_specs=[
            pl.BlockSpec((tm, B_KV_RANK), lambda i: (i, 0)),
            pl.BlockSpec((tm, 2 * B_ROPE), lambda i: (i, 0)),
            _const_spec(wuk.shape),
            _const_spec(wuv.shape),
            _const_spec((1, B_NOPE)),
        ],
        out_specs=[pl.BlockSpec((tm, B_HEADS * B_CAT), lambda i: (i, 0)), v_spec],
        out_shape=[jax.ShapeDtypeStruct((t, B_HEADS * B_CAT), BF16), v_shape],
        compiler_params=pltpu.CompilerParams(
            dimension_semantics=("arbitrary",), vmem_limit_bytes=VMEM_LIMIT),
        name="expand_b",
    )(ckv, kr2, wuk, wuv, gkn)


def _attn_bt_kernel(q_ref, k_ref, vt_ref, o_ref, s_ref, p_ref, alpha_ref, mask_ref, m_ref, acc_ref,
                    *, tq, n_blocks, kv_blocks, heads):
    pairs = [(qi, k0, min(kv_blocks, qi + 1 - k0))
             for qi in range(n_blocks) for k0 in range(0, qi + 1, kv_blocks)]

    key_chunk = lax.broadcasted_iota(jnp.int32, (tq, tq), 0) // CHUNK
    q_chunk = lax.broadcasted_iota(jnp.int32, (tq, tq), 1) // CHUNK
    mask_ref[...] = jnp.where(key_chunk <= q_chunk, 0.0, NEG_INF).astype(F32)

    half = tq // 2

    def stages(hh):
        s_h, p_h, alpha_h = s_ref.at[hh], p_ref.at[hh], alpha_ref.at[hh]
        m_h, acc_h = m_ref.at[hh], acc_ref.at[hh]

        def scores(t):
            qi, k0, nk = pairs[t]
            slot = t % 2
            q_t = q_ref[0, hh, :, qi * tq:(qi + 1) * tq]
            rows = nk * tq
            kb = k_ref[0, k0 * tq:k0 * tq + rows, hh * B_CAT:(hh + 1) * B_CAT]
            if k0 + nk - 1 == qi:
                full = rows - half
                s_h[slot, :full, :] = jnp.dot(kb[:full], q_t, preferred_element_type=F32)
                s_h[slot, full:rows, half:] = jnp.dot(kb[full:], q_t[:, half:], preferred_element_type=F32)
            else:
                s_h[slot, :rows, :] = jnp.dot(kb, q_t, preferred_element_type=F32)

        def regions(t):
            qi, k0, nk = pairs[t]
            rows = nk * tq
            if k0 + nk - 1 != qi:
                return [(0, rows, 0, tq, None)]
            base = rows - tq
            out = [(0, base, 0, tq, None)] if base else []
            return out + [(base, base + half, 0, half, 0), (base, base + half, half, tq, None),
                          (base + half, rows, half, tq, half)]

        def probabilities(t):
            qi, k0, nk = pairs[t]
            slot = t % 2
            parts = []
            for r0, r1, c0, c1, mrow in regions(t):
                s = s_h[slot, r0:r1, c0:c1]
                if mrow is not None:
                    s = s + mask_ref[mrow:mrow + (r1 - r0), c0:c1]
                parts.append(s)
            col_max = jnp.full((1, tq), NEG_INF, F32)
            for (r0, r1, c0, c1, _), s in zip(regions(t), parts):
                part_max = jnp.max(s, axis=0, keepdims=True)
                if c1 - c0 < tq:
                    pad = jnp.full((1, tq - (c1 - c0)), NEG_INF, F32)
                    part_max = jnp.concatenate([part_max, pad] if c0 == 0 else [pad, part_max], axis=1)
                col_max = jnp.maximum(col_max, part_max)
            if k0 == 0:
                m_new = col_max
            else:
                m_new = jnp.maximum(m_h[...], col_max)
                alpha_h[slot] = jnp.exp2(m_h[...] - m_new)
            m_h[...] = m_new
            for (r0, r1, c0, c1, _), s in zip(regions(t), parts):
                p_h[slot, r0:r1, c0:c1] = jnp.exp2(s - m_new[:, c0:c1]).astype(BF16)

        def values(t):
            qi, k0, nk = pairs[t]
            slot = t % 2
            diag = k0 + nk - 1 == qi

            def product(c0, c1, last_rows):
                total = None
                for b in range(nk):
                    n = last_rows if b == nk - 1 else tq
                    term = jnp.dot(vt_ref[0, hh, k0 + b, :, :n], p_h[slot, b * tq:b * tq + n, c0:c1],
                                   preferred_element_type=F32)
                    total = term if total is None else total + term
                return total

            if diag:
                pv = jnp.concatenate([product(0, half, half), product(half, tq, tq)], axis=1)
            else:
                pv = product(0, tq, tq)
            acc = pv if k0 == 0 else alpha_h[slot] * acc_h[...] + pv
            if diag:
                o_t = acc[:B_V, :] / acc[B_V:B_V + 1, :]
                o_ref[0, qi * tq:(qi + 1) * tq, hh * B_V:(hh + 1) * B_V] = o_t.T.astype(BF16)
            else:
                acc_h[...] = acc
        return scores, probabilities, values

    head_stages = [stages(hh) for hh in range(heads)]
    for scores, _, _ in head_stages:
        scores(0)
    for t in range(len(pairs) + 1):
        for scores, probabilities, values in head_stages:
            if t + 1 < len(pairs):
                scores(t + 1)
            if t < len(pairs):
                probabilities(t)
            if t >= 1:
                values(t - 1)


def _attn_bt(qt, kcat, vt, *, batch, seq, tq, kv_blocks, heads):
    k3 = kcat.reshape(batch, seq, B_HEADS * B_CAT)
    n_blocks = seq // tq
    out = pl.pallas_call(
        functools.partial(_attn_bt_kernel, tq=tq, n_blocks=n_blocks, kv_blocks=kv_blocks, heads=heads),
        grid=(batch, B_HEADS // heads),
        in_specs=[
            pl.BlockSpec((1, heads, B_CAT, seq), lambda b, h: (b, h, 0, 0)),
            pl.BlockSpec((1, seq, heads * B_CAT), lambda b, h: (b, 0, h)),
            pl.BlockSpec((1, heads, n_blocks, VT_ROWS, tq), lambda b, h: (b, h, 0, 0, 0)),
        ],
        out_specs=pl.BlockSpec((1, seq, heads * B_V), lambda b, h: (b, 0, h)),
        out_shape=jax.ShapeDtypeStruct((batch, seq, B_WIDTH), BF16),
        scratch_shapes=[pltpu.VMEM((heads, 2, kv_blocks * tq, tq), F32),
                        pltpu.VMEM((heads, 2, kv_blocks * tq, tq), BF16),
                        pltpu.VMEM((heads, 2, 1, tq), F32), pltpu.VMEM((tq, tq), F32),
                        pltpu.VMEM((heads, 1, tq), F32), pltpu.VMEM((heads, VT_ROWS, tq), F32)],
        compiler_params=pltpu.CompilerParams(
            dimension_semantics=("arbitrary", "arbitrary"), vmem_limit_bytes=VMEM_LIMIT),
        name="attn_bt",
    )(qt, k3, vt)
    return out.reshape(batch * seq, B_WIDTH)


def _attn_b_kernel(q_ref, k_ref, v_ref, o_ref):
    for hh in range(B_HEADS):
        q = q_ref[0, :, hh * B_CAT:(hh + 1) * B_CAT]
        kb = k_ref[0, :, hh * B_CAT:(hh + 1) * B_CAT]
        s = lax.dot_general(q, kb, _NT, preferred_element_type=F32)
        p = jnp.exp2(s - jnp.max(s, axis=-1, keepdims=True))
        l = jnp.sum(p, axis=-1, keepdims=True)
        vb = v_ref[0, :, hh * B_V:(hh + 1) * B_V]
        o = jnp.dot(p.astype(BF16), vb, preferred_element_type=F32) / l
        o_ref[0, :, hh * B_V:(hh + 1) * B_V] = o.astype(BF16)


def _attn_b(qcat, kcat, v, *, batch, q_len, k_len):
    q3 = qcat.reshape(batch, q_len, B_HEADS * B_CAT)
    k3 = kcat.reshape(batch, k_len, B_HEADS * B_CAT)
    v3 = v.reshape(batch, k_len, B_WIDTH)
    out = pl.pallas_call(
        _attn_b_kernel,
        grid=(batch,),
        in_specs=[
            pl.BlockSpec((1, q_len, B_HEADS * B_CAT), lambda b: (b, 0, 0)),
            pl.BlockSpec((1, k_len, B_HEADS * B_CAT), lambda b: (b, 0, 0)),
            pl.BlockSpec((1, k_len, B_WIDTH), lambda b: (b, 0, 0)),
        ],
        out_specs=pl.BlockSpec((1, q_len, B_WIDTH), lambda b: (b, 0, 0)),
        out_shape=jax.ShapeDtypeStruct((batch, q_len, B_WIDTH), BF16),
        compiler_params=pltpu.CompilerParams(
            dimension_semantics=("arbitrary",), vmem_limit_bytes=VMEM_LIMIT),
        name="attn_b",
    )(q3, k3, v3)
    return out.reshape(batch * q_len, B_WIDTH)


def _swap_halves(a):
    half = a.shape[-1] // 2
    return jnp.concatenate([a[..., half:], a[..., :half]], axis=-1)


def _rope_table(pos):
    half = B_ROPE // 2
    inv = ROPE_THETA ** (-jnp.arange(half, dtype=F32) / half)
    ang = pos.astype(F32)[:, None] * inv[None, :]
    cos, sin = jnp.cos(ang), jnp.sin(ang)
    return jnp.concatenate([cos, cos, -sin, sin], axis=-1)


def _row(v):
    return v.reshape(1, -1).astype(F32)


def _layer_a(x, k_cache, v_cache, ln, w_in, gq, gk, table, w_out, *, batch, seq, tm, tq, tm_out, defer_out):
    if k_cache is None:
        keep = min(A_WINDOW, seq)
        keep_rows, tiles_per_keep = keep, seq // tm
    else:
        keep = seq
        keep_rows, tiles_per_keep = tm, 1
    q, k, v, sg, k32, v32 = _proj_a(x, ln, w_in, gq, gk, tm=tm, ts=A_PROJ_SEG,
                                    keep_rows=keep_rows, tiles_per_keep=tiles_per_keep)
    bias = _band_bias(table, tq, mask_prefix=k_cache is None)
    o = _attn_a(q, k, v, bias, k_cache, v_cache, batch=batch, seq=seq, tq=tq,
                heads=1 if k_cache is None else A_HEADS)
    shape = (batch, keep, A_HEADS, A_HEAD_DIM)
    if defer_out:
        return x, (sg, o, w_out), k32.reshape(shape), v32.reshape(shape)
    y = _out_proj(x, sg, o, w_out, tm=tm_out)
    return y, None, k32.reshape(shape), v32.reshape(shape)


def _layer_b(x, pending, ckv_cache, kr_cache, pos, ln, w, wuq, gqa, gkva, gq, gkr, wuk, wuv, gkn, w_out,
             *, batch, seq, tm, tm_out):
    cs = _rope_table(pos)
    if ckv_cache is not None:
        cs = jnp.tile(cs, (batch, 1))
    outs = _proj_b(x, ln, w, wuq, gqa, gkva, gq, gkr, cs, tm=tm,
                   batch=batch if ckv_cache is None else None, pending=pending)
    if pending is not None:
        x, *outs = outs
    qcat, ckv, kr2, sg = outs
    if ckv_cache is None:
        kcat, vt = _expand_b(ckv, kr2, wuk, wuv.T, gkn, tm=B_ATTN_ROWS, batch=batch)
        o = _attn_bt(qcat, kcat, vt, batch=batch, seq=seq, tq=B_ATTN_ROWS, kv_blocks=B_KEY_BLOCKS, heads=1)
    else:
        k_len = ckv_cache.shape[1] + seq
        ckv_all = jnp.concatenate(
            [ckv_cache, ckv.reshape(batch, seq, B_KV_RANK)], axis=1).reshape(batch * k_len, B_KV_RANK)
        kr2_all = jnp.concatenate(
            [jnp.concatenate([kr_cache, kr_cache], axis=-1), kr2.reshape(batch, seq, 2 * B_ROPE)],
            axis=1).reshape(batch * k_len, 2 * B_ROPE)
        kcat, v = _expand_b(ckv_all, kr2_all, wuk, wuv, gkn, tm=B_EXPAND_ROWS)
        o = _attn_b(qcat, kcat, v, batch=batch, q_len=seq, k_len=k_len)
    y = _out_proj(x, sg, o, w_out, tm=tm_out)
    return y, ckv.reshape(batch, seq, B_KV_RANK), kr2[:, :B_ROPE].reshape(batch, seq, B_ROPE)


def kernel(x_prompt, x_sample, cache_a_k, cache_a_v, cache_b_ckv, cache_b_krope, a_ln, w_a_in, a_q_norm, a_k_norm, a_rel_bias, w_a_out, b_ln, w_b_in, b_q_a_norm, w_b_uq, b_kv_a_norm, w_b_uk, w_b_uv, b_q_nope_norm, b_k_nope_norm, b_q_rope_norm, b_k_rope_norm, w_b_out):
    batch, seq, _ = x_prompt.shape
    dec_batch, dec_seq, _ = x_sample.shape
    dec_tokens = dec_batch * dec_seq
    depth = a_ln.shape[0] + b_ln.shape[0]
    yp = x_prompt.reshape(batch * seq, D_MODEL)
    ys = x_sample.reshape(dec_tokens, D_MODEL)
    akp, avp, aks, avs, bcp, brp, bcs, brs = ([] for _ in range(8))
    for layer in range(depth):
        i = layer // 2
        if layer % 2 == 0:
            params = (_row(a_ln[i]), w_a_in[i].astype(BF16), _row(a_q_norm[i]), _row(a_k_norm[i]),
                      a_rel_bias[i], w_a_out[i].astype(BF16))
            yp, pending_p, k_p, v_p = _layer_a(yp, None, None, *params, batch=batch, seq=seq,
                                               tm=A_PROJ_ROWS, tq=A_Q_ROWS, tm_out=OUT_ROWS,
                                               defer_out=layer + 1 < depth)
            kc = cache_a_k[i].transpose(0, 2, 1, 3).astype(BF16)
            vc = cache_a_v[i].transpose(0, 2, 1, 3).astype(BF16)
            ys, _, k_s, v_s = _layer_a(ys, kc, vc, *params, batch=dec_batch, seq=dec_seq,
                                       tm=dec_tokens, tq=CHUNK, tm_out=dec_tokens, defer_out=False)
            akp.append(k_p); avp.append(v_p); aks.append(k_s); avs.append(v_s)
        else:
            w_in = w_b_in[i]
            c1, c2 = B_Q_RANK + B_KV_RANK, B_Q_RANK + B_KV_RANK + B_ROPE
            w = jnp.concatenate([w_in[:, :c2], _swap_halves(w_in[:, c1:c2]), w_in[:, c2:]], axis=1).astype(BF16)
            wuq = w_b_uq[i].reshape(B_Q_RANK, B_HEADS, B_QK)
            wuq_rope = wuq[..., B_NOPE:]
            wuq = jnp.concatenate([wuq[..., :B_NOPE], wuq_rope, _swap_halves(wuq_rope)],
                                  axis=-1).reshape(B_Q_RANK, B_HEADS * B_CAT).astype(BF16)
            g_qr, g_kr = b_q_rope_norm[i], b_k_rope_norm[i]
            gq = _row(jnp.concatenate([b_q_nope_norm[i], g_qr, _swap_halves(g_qr)]))
            gkr = _row(jnp.concatenate([g_kr, _swap_halves(g_kr)]))
            params = (_row(b_ln[i]), w, wuq, _row(b_q_a_norm[i]), _row(b_kv_a_norm[i]), gq, gkr,
                      w_b_uk[i].astype(BF16), w_b_uv[i].astype(BF16), _row(b_k_nope_norm[i]),
                      w_b_out[i].astype(BF16))
            yp, c_p, r_p = _layer_b(yp, pending_p, None, None, jnp.arange(seq), *params, batch=batch, seq=seq,
                                    tm=B_PROJ_ROWS if pending_p is None else B_FUSED_PROJ_ROWS, tm_out=OUT_ROWS)
            pending_p = None
            ys, c_s, r_s = _layer_b(ys, None, cache_b_ckv[i], cache_b_krope[i], PAST_LEN + jnp.arange(dec_seq),
                                    *params, batch=dec_batch, seq=dec_seq, tm=B_SAMPLE_PROJ_ROWS,
                                    tm_out=dec_tokens)
            bcp.append(c_p); brp.append(r_p); bcs.append(c_s); brs.append(r_s)
    yp = yp.reshape(batch, seq, D_MODEL)
    ys = ys.reshape(dec_batch, dec_seq, D_MODEL)
    return (yp, ys,
            jnp.stack(akp), jnp.stack(avp), jnp.stack(bcp), jnp.stack(brp),
            jnp.stack(aks), jnp.stack(avs), jnp.stack(bcs), jnp.stack(brs))
```

```python
import functools
import math

import numpy as np

import jax
import jax.numpy as jnp
from jax import lax
from jax.experimental import pallas as pl
from jax.experimental.pallas import tpu as pltpu

D_MODEL = 2048
CHUNK = 64
EPS = 1e-6
NEG_INF = -1e30
PAST_LEN = 1024
LOG2E = math.log2(math.e)

A_HEADS = 16
A_HEAD_DIM = 128
A_WIDTH = A_HEADS * A_HEAD_DIM
A_WINDOW = 8 * CHUNK
A_BAND = A_WINDOW + CHUNK
REL_CLIP = 128
A_SCALE = A_HEAD_DIM ** -0.5
A_PROJ_SEG = 512

B_HEADS = 16
B_NOPE = 128
B_ROPE = 64
B_V = 128
B_QK = B_NOPE + B_ROPE
B_Q_RANK = 512
B_KV_RANK = 512
B_WIDTH = B_HEADS * B_V
B_SCALE = B_QK ** -0.5
ROPE_THETA = 10000.0
B_CAT = B_NOPE + 2 * B_ROPE
B_LATENT_COLS = B_Q_RANK + B_KV_RANK + 2 * B_ROPE
BF16_SUBLANES = 16
VT_ROWS = B_V + BF16_SUBLANES

LANES = 128
VMEM_LIMIT = 56 * 1024 * 1024

A_PROJ_ROWS = 1024
A_Q_ROWS = 4 * CHUNK
OUT_ROWS = 512
B_PROJ_ROWS = 512
B_FUSED_PROJ_ROWS = 256
B_SAMPLE_PROJ_ROWS = 256
B_EXPAND_ROWS = 512
B_ATTN_ROWS = 512
B_KEY_BLOCKS = 2

F32 = jnp.float32
BF16 = jnp.bfloat16
_NT = (((1,), (1,)), ((), ()))


def _rms_scale(x):
    return lax.rsqrt(jnp.mean(x * x, axis=-1, keepdims=True) + EPS)


def _const_spec(shape):
    return pl.BlockSpec(shape, lambda *_: (0,) * len(shape), pipeline_mode=pl.Buffered(1))


def _proj_a_kernel(x_ref, ln_ref, wq_ref, wk_ref, wv_ref, wg_ref, gq_ref, gk_ref,
                   q_ref, k_ref, v_ref, sg_ref, k32_ref, v32_ref, h_ref, *, ts, tiles_per_keep, keep_rows):
    i = pl.program_id(0)
    j = pl.program_id(1)
    tm = x_ref.shape[0]

    @pl.when(j == 0)
    def _():
        x = x_ref[...]
        h_ref[...] = (x * _rms_scale(x) * ln_ref[...]).astype(BF16)

    h = h_ref[...]
    q = jnp.dot(h, wq_ref[...], preferred_element_type=F32)
    gq = gq_ref[...] * (A_SCALE * LOG2E)
    for hh in range(ts // LANES):
        sl = slice(hh * LANES, (hh + 1) * LANES)
        s = q[:, sl]
        q_ref[:, sl] = (s * _rms_scale(s) * gq).astype(BF16)

    k = jnp.dot(h, wk_ref[...], preferred_element_type=F32)
    gk = gk_ref[...]
    k_tails = []
    for hh in range(ts // LANES):
        sl = slice(hh * LANES, (hh + 1) * LANES)
        s = k[:, sl]
        kn = s * _rms_scale(s) * gk
        k_ref[:, sl] = kn.astype(BF16)
        k_tails.append(kn[tm - keep_rows:, :])

    v = jnp.dot(h, wv_ref[...], preferred_element_type=F32)
    v_ref[...] = v.astype(BF16)
    g = jnp.dot(h, wg_ref[...], preferred_element_type=F32)
    sg_ref[...] = (g / (1.0 + jnp.exp(-g))).astype(BF16)

    @pl.when((i % tiles_per_keep) == (tiles_per_keep - 1))
    def _():
        for hh, kn_tail in enumerate(k_tails):
            k32_ref[:, hh * LANES:(hh + 1) * LANES] = kn_tail
        v32_ref[...] = v[tm - keep_rows:, :]


def _proj_a(x, ln, w, gq, gk, *, tm, ts, keep_rows, tiles_per_keep):
    t = x.shape[0]
    nm = t // tm
    nkeep = nm // tiles_per_keep
    nj = A_WIDTH // ts

    def w_spec(part):
        return pl.BlockSpec((D_MODEL, ts), lambda i, j: (0, part * nj + j))

    def tail_map(i, j):
        is_tail = (i % tiles_per_keep) == (tiles_per_keep - 1)
        return (i // tiles_per_keep, jnp.where(is_tail, j, 0))

    kernel = functools.partial(_proj_a_kernel, ts=ts, tiles_per_keep=tiles_per_keep, keep_rows=keep_rows)
    act = jax.ShapeDtypeStruct((t, A_WIDTH), BF16)
    tail = jax.ShapeDtypeStruct((nkeep * keep_rows, A_WIDTH), F32)
    return pl.pallas_call(
        kernel,
        grid=(nm, nj),
        in_specs=[
            pl.BlockSpec((tm, D_MODEL), lambda i, j: (i, 0)),
            _const_spec((1, D_MODEL)),
            w_spec(0), w_spec(1), w_spec(2), w_spec(3),
            _const_spec((1, A_HEAD_DIM)),
            _const_spec((1, A_HEAD_DIM)),
        ],
        out_specs=[pl.BlockSpec((tm, ts), lambda i, j: (i, j))] * 4
                  + [pl.BlockSpec((keep_rows, ts), tail_map)] * 2,
        out_shape=[act] * 4 + [tail] * 2,
        scratch_shapes=[pltpu.VMEM((tm, D_MODEL), BF16)],
        compiler_params=pltpu.CompilerParams(
            dimension_semantics=("arbitrary", "arbitrary"), vmem_limit_bytes=VMEM_LIMIT),
        name="proj_a",
    )(x, ln, w, w, w, w, gq, gk)


def _attn_a_kernel(*refs, has_cache, tq, seq, n_bias, heads):
    if has_cache:
        q_ref, k_ref, v_ref, kc_ref, vc_ref, bias_ref, o_ref, kpad, vpad, s_ref = refs
    else:
        q_ref, k_ref, v_ref, bias_ref, o_ref, kpad, vpad, s_ref = refs
    width = A_WINDOW + tq
    n_blocks = seq // tq

    for hh in range(heads):
        cols = slice(hh * A_HEAD_DIM, (hh + 1) * A_HEAD_DIM)
        if has_cache:
            kpad[:, 0:A_WINDOW] = kc_ref[0, hh].T
            vpad[0:A_WINDOW, :] = vc_ref[0, hh]
        else:
            kpad[:, 0:A_WINDOW] = jnp.zeros((A_HEAD_DIM, A_WINDOW), BF16)
            vpad[0:A_WINDOW, :] = jnp.zeros((A_WINDOW, A_HEAD_DIM), BF16)
        kpad[:, A_WINDOW:A_WINDOW + seq] = k_ref[0, :, cols].T
        vpad[A_WINDOW:A_WINDOW + seq, :] = v_ref[0, :, cols]

        def scores(t):
            q = q_ref[0, t * tq:(t + 1) * tq, cols]
            kb_t = kpad[:, t * tq:t * tq + width]
            s_ref[t % 2] = jnp.dot(q, kb_t, preferred_element_type=F32)

        scores(0)
        for t in range(n_blocks):
            if t + 1 < n_blocks:
                scores(t + 1)
            s = s_ref[t % 2] + bias_ref[hh, min(t, n_bias - 1)]
            m = jnp.max(s, axis=-1, keepdims=True)
            p = jnp.exp2(s - m)
            l = jnp.sum(p, axis=-1, keepdims=True)
            vb = vpad[t * tq:t * tq + width, :]
            o = jnp.dot(p.astype(BF16), vb, preferred_element_type=F32) / l
            o_ref[0, t * tq:(t + 1) * tq, cols] = o.astype(BF16)


def _attn_a(q, k, v, bias, k_cache, v_cache, *, batch, seq, tq, heads):
    has_cache = k_cache is not None
    width = A_WINDOW + tq
    n_bias = bias.shape[1]
    in_specs = [pl.BlockSpec((1, seq, heads * A_HEAD_DIM), lambda b, h: (b, 0, h))] * 3
    args = [a.reshape(batch, seq, A_WIDTH) for a in (q, k, v)]
    if has_cache:
        in_specs += [pl.BlockSpec((1, heads, A_WINDOW, A_HEAD_DIM), lambda b, h: (b, h, 0, 0))] * 2
        args += [k_cache, v_cache]
    in_specs.append(pl.BlockSpec((heads, n_bias, tq, width), lambda b, h: (h, 0, 0, 0)))
    args.append(bias)
    kernel = functools.partial(_attn_a_kernel, has_cache=has_cache, tq=tq, seq=seq, n_bias=n_bias, heads=heads)
    out = pl.pallas_call(
        kernel,
        grid=(batch, A_HEADS // heads),
        in_specs=in_specs,
        out_specs=pl.BlockSpec((1, seq, heads * A_HEAD_DIM), lambda b, h: (b, 0, h)),
        out_shape=jax.ShapeDtypeStruct((batch, seq, A_WIDTH), BF16),
        scratch_shapes=[pltpu.VMEM((A_HEAD_DIM, A_WINDOW + seq), BF16),
                        pltpu.VMEM((A_WINDOW + seq, A_HEAD_DIM), BF16),
                        pltpu.VMEM((2, tq, width), F32)],
        compiler_params=pltpu.CompilerParams(
            dimension_semantics=("arbitrary", "arbitrary"), vmem_limit_bytes=VMEM_LIMIT),
        name="attn_a",
    )(*args)
    return out.reshape(batch * seq, A_WIDTH)


def _band_bias(table, tq, mask_prefix):
    width = A_WINDOW + tq
    length = CHUNK + A_BAND - 1
    n_table = 2 * REL_CLIP + 1
    c0 = A_WINDOW + REL_CLIP - (A_BAND - 1)
    n_lo, first = max(0, -c0), max(0, c0)
    n_hi = length - n_lo - (n_table - first)
    heads = table.shape[0]
    v = jnp.concatenate([jnp.broadcast_to(table[:, :1], (heads, n_lo)), table[:, first:],
                         jnp.broadcast_to(table[:, -1:], (heads, n_hi))], axis=1).astype(F32)
    w = jnp.pad(v[:, ::-1], ((0, 0), (0, 1)))
    skew = jnp.tile(w, (1, CHUNK))[:, :CHUNK * length].reshape(heads, CHUNK, length)
    band = skew[:, :, CHUNK - 1:CHUNK - 1 + A_BAND] * LOG2E
    chunks = [jnp.pad(band, ((0, 0), (0, 0), (c * CHUNK, tq - CHUNK - c * CHUNK)), constant_values=NEG_INF)
              for c in range(tq // CHUNK)]
    block = jnp.concatenate(chunks, axis=1)
    if not mask_prefix:
        return block[:, None]
    j = np.arange(width)[None, :]
    keep = np.stack([np.broadcast_to(j + v_idx * tq >= A_WINDOW, (tq, width))
                     for v_idx in range(A_WINDOW // tq + 1)])
    return jnp.where(keep[None], block[:, None], NEG_INF)


def _out_proj_kernel(x_ref, sg_ref, o_ref, w_ref, y_ref):
    gated = sg_ref[...] * o_ref[...]
    y_ref[...] = x_ref[...] + jnp.dot(gated, w_ref[...], preferred_element_type=F32)


def _out_proj(x, sg, o, w, *, tm):
    t, width = o.shape
    return pl.pallas_call(
        _out_proj_kernel,
        grid=(t // tm,),
        in_specs=[
            pl.BlockSpec((tm, D_MODEL), lambda i: (i, 0)),
            pl.BlockSpec((tm, width), lambda i: (i, 0)),
            pl.BlockSpec((tm, width), lambda i: (i, 0)),
            _const_spec((width, D_MODEL)),
        ],
        out_specs=pl.BlockSpec((tm, D_MODEL), lambda i: (i, 0)),
        out_shape=jax.ShapeDtypeStruct((t, D_MODEL), F32),
        compiler_params=pltpu.CompilerParams(
            dimension_semantics=("arbitrary",), vmem_limit_bytes=VMEM_LIMIT),
        name="out_proj",
    )(x, sg, o, w)


Q_HEADS_PER_DOT = 4
GATE_CHUNK = 512


def _proj_b_kernel(*refs, transposed_q, fused_residual):
    if fused_residual:
        (x_ref, sgp_ref, op_ref, wout_ref, ln_ref, w_ref, wuq_ref, gqa_ref, gkva_ref, gq_ref, gkr_ref, cs_ref,
         csq_ref, y_ref, q_ref, ckv_ref, kr2_ref, sg_ref) = refs
        x = x_ref[...] + jnp.dot(sgp_ref[...] * op_ref[...], wout_ref[...], preferred_element_type=F32)
        y_ref[...] = x
    else:
        (x_ref, ln_ref, w_ref, wuq_ref, gqa_ref, gkva_ref, gq_ref, gkr_ref, cs_ref, csq_ref,
         q_ref, ckv_ref, kr2_ref, sg_ref) = refs
        x = x_ref[...]
    h = (x * _rms_scale(x) * ln_ref[...]).astype(BF16)
    t1 = jnp.dot(h, w_ref[:, :B_LATENT_COLS], preferred_element_type=F32)
    cq = t1[:, :B_Q_RANK]
    ckv = t1[:, B_Q_RANK:B_Q_RANK + B_KV_RANK]
    kr = t1[:, B_Q_RANK + B_KV_RANK:]

    ckv_ref[...] = ckv * _rms_scale(ckv) * gkva_ref[...]
    wk = kr * _rms_scale(kr) * gkr_ref[...] * cs_ref[...]
    kr2_ref[...] = wk + pltpu.roll(wk, B_ROPE, axis=1)

    cqn = (cq * _rms_scale(cq) * gqa_ref[...]).astype(BF16)
    if transposed_q:
        g_nope = gq_ref[:B_NOPE, :] * (B_SCALE * LOG2E)
        g_rope = gq_ref[B_NOPE:, :] * csq_ref[...] * (B_SCALE * LOG2E)
        rows = Q_HEADS_PER_DOT * B_CAT
        for grp in range(B_HEADS // Q_HEADS_PER_DOT):
            qt = lax.dot_general(wuq_ref[grp * rows:(grp + 1) * rows, :], cqn, _NT, preferred_element_type=F32)
            for hh in range(Q_HEADS_PER_DOT):
                head = grp * Q_HEADS_PER_DOT + hh
                qn = qt[hh * B_CAT:hh * B_CAT + B_NOPE, :]
                qr = qt[hh * B_CAT + B_NOPE:(hh + 1) * B_CAT, :]
                sn = lax.rsqrt(jnp.mean(qn * qn, axis=0, keepdims=True) + EPS)
                sr = lax.rsqrt(jnp.mean(qr * qr, axis=0, keepdims=True) + EPS)
                q_ref[0, head, 0:B_NOPE, :] = (qn * sn * g_nope).astype(BF16)
                q_ref[0, head, B_NOPE:B_CAT, :] = (qr * sr * g_rope).astype(BF16)
    else:
        q = jnp.dot(cqn, wuq_ref[...], preferred_element_type=F32)
        g_nope = gq_ref[:, :B_NOPE] * (B_SCALE * LOG2E)
        g_rope = gq_ref[:, B_NOPE:] * csq_ref[...] * (B_SCALE * LOG2E)
        for hh in range(B_HEADS):
            qn = q[:, hh * B_CAT:hh * B_CAT + B_NOPE]
            qr = q[:, hh * B_CAT + B_NOPE:(hh + 1) * B_CAT]
            q_ref[:, hh * B_CAT:hh * B_CAT + B_NOPE] = (qn * _rms_scale(qn) * g_nope).astype(BF16)
            q_ref[:, hh * B_CAT + B_NOPE:(hh + 1) * B_CAT] = (qr * _rms_scale(qr) * g_rope).astype(BF16)

    for c in range(B_WIDTH // GATE_CHUNK):
        sl = slice(c * GATE_CHUNK, (c + 1) * GATE_CHUNK)
        g = jnp.dot(h, w_ref[:, B_LATENT_COLS + c * GATE_CHUNK:B_LATENT_COLS + (c + 1) * GATE_CHUNK],
                    preferred_element_type=F32)
        sg_ref[:, sl] = (g / (1.0 + jnp.exp(-g))).astype(BF16)


def _proj_b(x, ln, w, wuq, gqa, gkva, gq, gkr, cs, *, tm, batch=None, pending=None):
    t = x.shape[0]
    n_pos_tiles = cs.shape[0] // tm
    transposed_q = batch is not None
    pos_spec = pl.BlockSpec((tm, 2 * B_ROPE), lambda i: (i % n_pos_tiles, 0))
    if transposed_q:
        tiles = t // batch // tm
        wuq = wuq.T
        gq = jnp.broadcast_to(gq.reshape(B_CAT, 1), (B_CAT, tm))
        csq = cs.T
        gq_spec = _const_spec((B_CAT, tm))
        csq_spec = pl.BlockSpec((2 * B_ROPE, tm), lambda i: (0, i % n_pos_tiles))
        q_spec = pl.BlockSpec((1, B_HEADS, B_CAT, tm), lambda i: (i // tiles, 0, 0, i % tiles))
        q_shape = jax.ShapeDtypeStruct((batch, B_HEADS, B_CAT, t // batch), BF16)
    else:
        csq = cs
        gq_spec = _const_spec((1, B_CAT))
        csq_spec = pos_spec
        q_spec = pl.BlockSpec((tm, B_HEADS * B_CAT), lambda i: (i, 0))
        q_shape = jax.ShapeDtypeStruct((t, B_HEADS * B_CAT), BF16)
    row_spec = lambda width: pl.BlockSpec((tm, width), lambda i: (i, 0))
    in_specs = [row_spec(D_MODEL)]
    args = [x]
    out_specs, out_shape = [], []
    if pending is not None:
        sg_prev, o_prev, w_out = pending
        in_specs += [row_spec(sg_prev.shape[1]), row_spec(o_prev.shape[1]), _const_spec(w_out.shape)]
        args += [sg_prev, o_prev, w_out]
        out_specs.append(row_spec(D_MODEL))
        out_shape.append(jax.ShapeDtypeStruct((t, D_MODEL), F32))
    in_specs += [
        _const_spec((1, D_MODEL)),
        _const_spec(w.shape),
        _const_spec(wuq.shape),
        _const_spec((1, B_Q_RANK)),
        _const_spec((1, B_KV_RANK)),
        gq_spec,
        _const_spec((1, 2 * B_ROPE)),
        pos_spec,
        csq_spec,
    ]
    args += [ln, w, wuq, gqa, gkva, gq, gkr, cs, csq]
    out_specs += [q_spec, row_spec(B_KV_RANK), row_spec(2 * B_ROPE), row_spec(B_WIDTH)]
    out_shape += [
        q_shape,
        jax.ShapeDtypeStruct((t, B_KV_RANK), F32),
        jax.ShapeDtypeStruct((t, 2 * B_ROPE), F32),
        jax.ShapeDtypeStruct((t, B_WIDTH), BF16),
    ]
    return pl.pallas_call(
        functools.partial(_proj_b_kernel, transposed_q=transposed_q, fused_residual=pending is not None),
        grid=(t // tm,),
        in_specs=in_specs,
        out_specs=out_specs,
        out_shape=out_shape,
        compiler_params=pltpu.CompilerParams(
            dimension_semantics=("arbitrary",), vmem_limit_bytes=VMEM_LIMIT),
        name="proj_b",
    )(*args)


def _expand_b_kernel(ckv_ref, kr2_ref, wuk_ref, wuv_ref, gkn_ref, kcat_ref, v_ref, *, transposed_v):
    c = ckv_ref[...].astype(BF16)
    kn = jnp.dot(c, wuk_ref[...], preferred_element_type=F32)
    krb = kr2_ref[...].astype(BF16)
    g = gkn_ref[...]
    for hh in range(B_HEADS):
        s = kn[:, hh * B_NOPE:(hh + 1) * B_NOPE]
        kcat_ref[:, hh * B_CAT:hh * B_CAT + B_NOPE] = (s * _rms_scale(s) * g).astype(BF16)
        kcat_ref[:, hh * B_CAT + B_NOPE:(hh + 1) * B_CAT] = krb
    if transposed_v:
        vt = lax.dot_general(wuv_ref[...], c, _NT, preferred_element_type=F32).astype(BF16)
        tm = c.shape[0]
        for hh in range(B_HEADS):
            v_ref[0, hh, 0, 0:B_V, :] = vt[hh * B_V:(hh + 1) * B_V, :]
            v_ref[0, hh, 0, B_V:VT_ROWS, :] = jnp.ones((VT_ROWS - B_V, tm), BF16)
    else:
        v_ref[...] = jnp.dot(c, wuv_ref[...], preferred_element_type=F32).astype(BF16)


def _expand_b(ckv, kr2, wuk, wuv, gkn, *, tm, batch=None):
    t = ckv.shape[0]
    transposed_v = batch is not None
    if transposed_v:
        tiles = t // batch // tm
        v_spec = pl.BlockSpec((1, B_HEADS, 1, VT_ROWS, tm), lambda i: (i // tiles, 0, i % tiles, 0, 0))
        v_shape = jax.ShapeDtypeStruct((batch, B_HEADS, tiles, VT_ROWS, tm), BF16)
    else:
        v_spec = pl.BlockSpec((tm, B_WIDTH), lambda i: (i, 0))
        v_shape = jax.ShapeDtypeStruct((t, B_WIDTH), BF16)
    return pl.pallas_call(
        functools.partial(_expand_b_kernel, transposed_v=transposed_v),
        grid=(t // tm,),
        in_specs=[
            pl.BlockSpec((tm, B_KV_RANK), lambda i: (i, 0)),
            pl.BlockSpec((tm, 2 * B_ROPE), lambda i: (i, 0)),
            _const_spec(wuk.shape),
            _const_spec(wuv.shape),
            _const_spec((1, B_NOPE)),
        ],
        out_specs=[pl.BlockSpec((tm, B_HEADS * B_CAT), lambda i: (i, 0)), v_spec],
        out_shape=[jax.ShapeDtypeStruct((t, B_HEADS * B_CAT), BF16), v_shape],
        compiler_params=pltpu.CompilerParams(
            dimension_semantics=("arbitrary",), vmem_limit_bytes=VMEM_LIMIT),
        name="expand_b",
    )(ckv, kr2, wuk, wuv, gkn)


def _attn_bt_kernel(q_ref, k_ref, vt_ref, o_ref, s_ref, p_ref, alpha_ref, mask_ref, m_ref, acc_ref,
                    *, tq, n_blocks, kv_blocks, heads):
    pairs = [(qi, k0, min(kv_blocks, qi + 1 - k0))
             for qi in range(n_blocks) for k0 in range(0, qi + 1, kv_blocks)]

    key_chunk = lax.broadcasted_iota(jnp.int32, (tq, tq), 0) // CHUNK
    q_chunk = lax.broadcasted_iota(jnp.int32, (tq, tq), 1) // CHUNK
    mask_ref[...] = jnp.where(key_chunk <= q_chunk, 0.0, NEG_INF).astype(F32)

    half = tq // 2

    def stages(hh):
        s_h, p_h, alpha_h = s_ref.at[hh], p_ref.at[hh], alpha_ref.at[hh]
        m_h, acc_h = m_ref.at[hh], acc_ref.at[hh]

        def scores(t):
            qi, k0, nk = pairs[t]
            slot = t % 2
            q_t = q_ref[0, hh, :, qi * tq:(qi + 1) * tq]
            rows = nk * tq
            kb = k_ref[0, k0 * tq:k0 * tq + rows, hh * B_CAT:(hh + 1) * B_CAT]
            if k0 + nk - 1 == qi:
                full = rows - half
                s_h[slot, :full, :] = jnp.dot(kb[:full], q_t, preferred_element_type=F32)
                s_h[slot, full:rows, half:] = jnp.dot(kb[full:], q_t[:, half:], preferred_element_type=F32)
            else:
                s_h[slot, :rows, :] = jnp.dot(kb, q_t, preferred_element_type=F32)

        def regions(t):
            qi, k0, nk = pairs[t]
            rows = nk * tq
            if k0 + nk - 1 != qi:
                return [(0, rows, 0, tq, None)]
            base = rows - tq
            out = [(0, base, 0, tq, None)] if base else []
            return out + [(base, base + half, 0, half, 0), (base, base + half, half, tq, None),
                          (base + half, rows, half, tq, half)]

        def probabilities(t):
            qi, k0, nk = pairs[t]
            slot = t % 2
            parts = []
            for r0, r1, c0, c1, mrow in regions(t):
                s = s_h[slot, r0:r1, c0:c1]
                if mrow is not None:
                    s = s + mask_ref[mrow:mrow + (r1 - r0), c0:c1]
                parts.append(s)
            col_max = jnp.full((1, tq), NEG_INF, F32)
            for (r0, r1, c0, c1, _), s in zip(regions(t), parts):
                part_max = jnp.max(s, axis=0, keepdims=True)
                if c1 - c0 < tq:
                    pad = jnp.full((1, tq - (c1 - c0)), NEG_INF, F32)
                    part_max = jnp.concatenate([part_max, pad] if c0 == 0 else [pad, part_max], axis=1)
                col_max = jnp.maximum(col_max, part_max)
            if k0 == 0:
                m_new = col_max
            else:
                m_new = jnp.maximum(m_h[...], col_max)
                alpha_h[slot] = jnp.exp2(m_h[...] - m_new)
            m_h[...] = m_new
            for (r0, r1, c0, c1, _), s in zip(regions(t), parts):
                p_h[slot, r0:r1, c0:c1] = jnp.exp2(s - m_new[:, c0:c1]).astype(BF16)

        def values(t):
            qi, k0, nk = pairs[t]
            slot = t % 2
            diag = k0 + nk - 1 == qi

            def product(c0, c1, last_rows):
                total = None
                for b in range(nk):
                    n = last_rows if b == nk - 1 else tq
                    term = jnp.dot(vt_ref[0, hh, k0 + b, :, :n], p_h[slot, b * tq:b * tq + n, c0:c1],
                                   preferred_element_type=F32)
                    total = term if total is None else total + term
                return total

            if diag:
                pv = jnp.concatenate([product(0, half, half), product(half, tq, tq)], axis=1)
            else:
                pv = product(0, tq, tq)
            acc = pv if k0 == 0 else alpha_h[slot] * acc_h[...] + pv
            if diag:
                o_t = acc[:B_V, :] / acc[B_V:B_V + 1, :]
                o_ref[0, qi * tq:(qi + 1) * tq, hh * B_V:(hh + 1) * B_V] = o_t.T.astype(BF16)
            else:
                acc_h[...] = acc
        return scores, probabilities, values

    head_stages = [stages(hh) for hh in range(heads)]
    for scores, _, _ in head_stages:
        scores(0)
    for t in range(len(pairs) + 1):
        for scores, probabilities, values in head_stages:
            if t + 1 < len(pairs):
                scores(t + 1)
            if t < len(pairs):
                probabilities(t)
            if t >= 1:
                values(t - 1)


def _attn_bt(qt, kcat, vt, *, batch, seq, tq, kv_blocks, heads):
    k3 = kcat.reshape(batch, seq, B_HEADS * B_CAT)
    n_blocks = seq // tq
    out = pl.pallas_call(
        functools.partial(_attn_bt_kernel, tq=tq, n_blocks=n_blocks, kv_blocks=kv_blocks, heads=heads),
        grid=(batch, B_HEADS // heads),
        in_specs=[
            pl.BlockSpec((1, heads, B_CAT, seq), lambda b, h: (b, h, 0, 0)),
            pl.BlockSpec((1, seq, heads * B_CAT), lambda b, h: (b, 0, h)),
            pl.BlockSpec((1, heads, n_blocks, VT_ROWS, tq), lambda b, h: (b, h, 0, 0, 0)),
        ],
        out_specs=pl.BlockSpec((1, seq, heads * B_V), lambda b, h: (b, 0, h)),
        out_shape=jax.ShapeDtypeStruct((batch, seq, B_WIDTH), BF16),
        scratch_shapes=[pltpu.VMEM((heads, 2, kv_blocks * tq, tq), F32),
                        pltpu.VMEM((heads, 2, kv_blocks * tq, tq), BF16),
                        pltpu.VMEM((heads, 2, 1, tq), F32), pltpu.VMEM((tq, tq), F32),
                        pltpu.VMEM((heads, 1, tq), F32), pltpu.VMEM((heads, VT_ROWS, tq), F32)],
        compiler_params=pltpu.CompilerParams(
            dimension_semantics=("arbitrary", "arbitrary"), vmem_limit_bytes=VMEM_LIMIT),
        name="attn_bt",
    )(qt, k3, vt)
    return out.reshape(batch * seq, B_WIDTH)


def _attn_b_kernel(q_ref, k_ref, v_ref, o_ref):
    for hh in range(B_HEADS):
        q = q_ref[0, :, hh * B_CAT:(hh + 1) * B_CAT]
        kb = k_ref[0, :, hh * B_CAT:(hh + 1) * B_CAT]
        s = lax.dot_general(q, kb, _NT, preferred_element_type=F32)
        p = jnp.exp2(s - jnp.max(s, axis=-1, keepdims=True))
        l = jnp.sum(p, axis=-1, keepdims=True)
        vb = v_ref[0, :, hh * B_V:(hh + 1) * B_V]
        o = jnp.dot(p.astype(BF16), vb, preferred_element_type=F32) / l
        o_ref[0, :, hh * B_V:(hh + 1) * B_V] = o.astype(BF16)


def _attn_b(qcat, kcat, v, *, batch, q_len, k_len):
    q3 = qcat.reshape(batch, q_len, B_HEADS * B_CAT)
    k3 = kcat.reshape(batch, k_len, B_HEADS * B_CAT)
    v3 = v.reshape(batch, k_len, B_WIDTH)
    out = pl.pallas_call(
        _attn_b_kernel,
        grid=(batch,),
        in_specs=[
            pl.BlockSpec((1, q_len, B_HEADS * B_CAT), lambda b: (b, 0, 0)),
            pl.BlockSpec((1, k_len, B_HEADS * B_CAT), lambda b: (b, 0, 0)),
            pl.BlockSpec((1, k_len, B_WIDTH), lambda b: (b, 0, 0)),
        ],
        out_specs=pl.BlockSpec((1, q_len, B_WIDTH), lambda b: (b, 0, 0)),
        out_shape=jax.ShapeDtypeStruct((batch, q_len, B_WIDTH), BF16),
        compiler_params=pltpu.CompilerParams(
            dimension_semantics=("arbitrary",), vmem_limit_bytes=VMEM_LIMIT),
        name="attn_b",
    )(q3, k3, v3)
    return out.reshape(batch * q_len, B_WIDTH)


def _swap_halves(a):
    half = a.shape[-1] // 2
    return jnp.concatenate([a[..., half:], a[..., :half]], axis=-1)


def _rope_table(pos):
    half = B_ROPE // 2
    inv = ROPE_THETA ** (-jnp.arange(half, dtype=F32) / half)
    ang = pos.astype(F32)[:, None] * inv[None, :]
    cos, sin = jnp.cos(ang), jnp.sin(ang)
    return jnp.concatenate([cos, cos, -sin, sin], axis=-1)


def _row(v):
    return v.reshape(1, -1).astype(F32)


def _layer_a(x, k_cache, v_cache, ln, w_in, gq, gk, table, w_out, *, batch, seq, tm, tq, tm_out, defer_out):
    if k_cache is None:
        keep = min(A_WINDOW, seq)
        keep_rows, tiles_per_keep = keep, seq // tm
    else:
        keep = seq
        keep_rows, tiles_per_keep = tm, 1
    q, k, v, sg, k32, v32 = _proj_a(x, ln, w_in, gq, gk, tm=tm, ts=A_PROJ_SEG,
                                    keep_rows=keep_rows, tiles_per_keep=tiles_per_keep)
    bias = _band_bias(table, tq, mask_prefix=k_cache is None)
    o = _attn_a(q, k, v, bias, k_cache, v_cache, batch=batch, seq=seq, tq=tq,
                heads=1 if k_cache is None else A_HEADS)
    shape = (batch, keep, A_HEADS, A_HEAD_DIM)
    if defer_out:
        return x, (sg, o, w_out), k32.reshape(shape), v32.reshape(shape)
    y = _out_proj(x, sg, o, w_out, tm=tm_out)
    return y, None, k32.reshape(shape), v32.reshape(shape)


def _layer_b(x, pending, ckv_cache, kr_cache, pos, ln, w, wuq, gqa, gkva, gq, gkr, wuk, wuv, gkn, w_out,
             *, batch, seq, tm, tm_out):
    cs = _rope_table(pos)
    if ckv_cache is not None:
        cs = jnp.tile(cs, (batch, 1))
    outs = _proj_b(x, ln, w, wuq, gqa, gkva, gq, gkr, cs, tm=tm,
                   batch=batch if ckv_cache is None else None, pending=pending)
    if pending is not None:
        x, *outs = outs
    qcat, ckv, kr2, sg = outs
    if ckv_cache is None:
        kcat, vt = _expand_b(ckv, kr2, wuk, wuv.T, gkn, tm=B_ATTN_ROWS, batch=batch)
        o = _attn_bt(qcat, kcat, vt, batch=batch, seq=seq, tq=B_ATTN_ROWS, kv_blocks=B_KEY_BLOCKS, heads=1)
    else:
        k_len = ckv_cache.shape[1] + seq
        ckv_all = jnp.concatenate(
            [ckv_cache, ckv.reshape(batch, seq, B_KV_RANK)], axis=1).reshape(batch * k_len, B_KV_RANK)
        kr2_all = jnp.concatenate(
            [jnp.concatenate([kr_cache, kr_cache], axis=-1), kr2.reshape(batch, seq, 2 * B_ROPE)],
            axis=1).reshape(batch * k_len, 2 * B_ROPE)
        kcat, v = _expand_b(ckv_all, kr2_all, wuk, wuv, gkn, tm=B_EXPAND_ROWS)
        o = _attn_b(qcat, kcat, v, batch=batch, q_len=seq, k_len=k_len)
    y = _out_proj(x, sg, o, w_out, tm=tm_out)
    return y, ckv.reshape(batch, seq, B_KV_RANK), kr2[:, :B_ROPE].reshape(batch, seq, B_ROPE)


def kernel(x_prompt, x_sample, cache_a_k, cache_a_v, cache_b_ckv, cache_b_krope, a_ln, w_a_in, a_q_norm, a_k_norm, a_rel_bias, w_a_out, b_ln, w_b_in, b_q_a_norm, w_b_uq, b_kv_a_norm, w_b_uk, w_b_uv, b_q_nope_norm, b_k_nope_norm, b_q_rope_norm, b_k_rope_norm, w_b_out):
    batch, seq, _ = x_prompt.shape
    dec_batch, dec_seq, _ = x_sample.shape
    dec_tokens = dec_batch * dec_seq
    depth = a_ln.shape[0] + b_ln.shape[0]
    yp = x_prompt.reshape(batch * seq, D_MODEL)
    ys = x_sample.reshape(dec_tokens, D_MODEL)
    akp, avp, aks, avs, bcp, brp, bcs, brs = ([] for _ in range(8))
    for layer in range(depth):
        i = layer // 2
        if layer % 2 == 0:
            params = (_row(a_ln[i]), w_a_in[i].astype(BF16), _row(a_q_norm[i]), _row(a_k_norm[i]),
                      a_rel_bias[i], w_a_out[i].astype(BF16))
            yp, pending_p, k_p, v_p = _layer_a(yp, None, None, *params, batch=batch, seq=seq,
                                               tm=A_PROJ_ROWS, tq=A_Q_ROWS, tm_out=OUT_ROWS,
                                               defer_out=layer + 1 < depth)
            kc = cache_a_k[i].transpose(0, 2, 1, 3).astype(BF16)
            vc = cache_a_v[i].transpose(0, 2, 1, 3).astype(BF16)
            ys, _, k_s, v_s = _layer_a(ys, kc, vc, *params, batch=dec_batch, seq=dec_seq,
                                       tm=dec_tokens, tq=CHUNK, tm_out=dec_tokens, defer_out=False)
            akp.append(k_p); avp.append(v_p); aks.append(k_s); avs.append(v_s)
        else:
            w_in = w_b_in[i]
            c1, c2 = B_Q_RANK + B_KV_RANK, B_Q_RANK + B_KV_RANK + B_ROPE
            w = jnp.concatenate([w_in[:, :c2], _swap_halves(w_in[:, c1:c2]), w_in[:, c2:]], axis=1).astype(BF16)
            wuq = w_b_uq[i].reshape(B_Q_RANK, B_HEADS, B_QK)
            wuq_rope = wuq[..., B_NOPE:]
            wuq = jnp.concatenate([wuq[..., :B_NOPE], wuq_rope, _swap_halves(wuq_rope)],
                                  axis=-1).reshape(B_Q_RANK, B_HEADS * B_CAT).astype(BF16)
            g_qr, g_kr = b_q_rope_norm[i], b_k_rope_norm[i]
            gq = _row(jnp.concatenate([b_q_nope_norm[i], g_qr, _swap_halves(g_qr)]))
            gkr = _row(jnp.concatenate([g_kr, _swap_halves(g_kr)]))
            params = (_row(b_ln[i]), w, wuq, _row(b_q_a_norm[i]), _row(b_kv_a_norm[i]), gq, gkr,
                      w_b_uk[i].astype(BF16), w_b_uv[i].astype(BF16), _row(b_k_nope_norm[i]),
                      w_b_out[i].astype(BF16))
            yp, c_p, r_p = _layer_b(yp, pending_p, None, None, jnp.arange(seq), *params, batch=batch, seq=seq,
                                    tm=B_PROJ_ROWS if pending_p is None else B_FUSED_PROJ_ROWS, tm_out=OUT_ROWS)
            pending_p = None
            ys, c_s, r_s = _layer_b(ys, None, cache_b_ckv[i], cache_b_krope[i], PAST_LEN + jnp.arange(dec_seq),
                                    *params, batch=dec_batch, seq=dec_seq, tm=B_SAMPLE_PROJ_ROWS,
                                    tm_out=dec_tokens)
            bcp.append(c_p); brp.append(r_p); bcs.append(c_s); brs.append(r_s)
    yp = yp.reshape(batch, seq, D_MODEL)
    ys = ys.reshape(dec_batch, dec_seq, D_MODEL)
    return (yp, ys,
            jnp.stack(akp), jnp.stack(avp), jnp.stack(bcp), jnp.stack(brp),
            jnp.stack(aks), jnp.stack(avs), jnp.stack(bcs), jnp.stack(brs))
```

```python
import functools
import math

import numpy as np

import jax
import jax.numpy as jnp
from jax import lax
from jax.experimental import pallas as pl
from jax.experimental.pallas import tpu as pltpu

D_MODEL = 2048
CHUNK = 64
EPS = 1e-6
NEG_INF = -1e30
PAST_LEN = 1024
LOG2E = math.log2(math.e)

A_HEADS = 16
A_HEAD_DIM = 128
A_WIDTH = A_HEADS * A_HEAD_DIM
A_WINDOW = 8 * CHUNK
A_BAND = A_WINDOW + CHUNK
REL_CLIP = 128
A_SCALE = A_HEAD_DIM ** -0.5
A_PROJ_SEG = 512

B_HEADS = 16
B_NOPE = 128
B_ROPE = 64
B_V = 128
B_QK = B_NOPE + B_ROPE
B_Q_RANK = 512
B_KV_RANK = 512
B_WIDTH = B_HEADS * B_V
B_SCALE = B_QK ** -0.5
ROPE_THETA = 10000.0
B_CAT = B_NOPE + 2 * B_ROPE
B_LATENT_COLS = B_Q_RANK + B_KV_RANK + 2 * B_ROPE
BF16_SUBLANES = 16
VT_ROWS = B_V + BF16_SUBLANES

LANES = 128
VMEM_LIMIT = 56 * 1024 * 1024

A_PROJ_ROWS = 1024
A_Q_ROWS = 4 * CHUNK
OUT_ROWS = 512
B_PROJ_ROWS = 512
B_FUSED_PROJ_ROWS = 256
B_SAMPLE_PROJ_ROWS = 256
B_EXPAND_ROWS = 512
B_ATTN_ROWS = 512
B_KEY_BLOCKS = 2

F32 = jnp.float32
BF16 = jnp.bfloat16
_NT = (((1,), (1,)), ((), ()))


def _rms_scale(x):
    return lax.rsqrt(jnp.mean(x * x, axis=-1, keepdims=True) + EPS)


def _const_spec(shape):
    return pl.BlockSpec(shape, lambda *_: (0,) * len(shape), pipeline_mode=pl.Buffered(1))


def _proj_a_kernel(x_ref, ln_ref, wq_ref, wk_ref, wv_ref, wg_ref, gq_ref, gk_ref,
                   q_ref, k_ref, v_ref, sg_ref, k32_ref, v32_ref, h_ref, *, ts, tiles_per_keep, keep_rows):
    i = pl.program_id(0)
    j = pl.program_id(1)
    tm = x_ref.shape[0]

    @pl.when(j == 0)
    def _():
        x = x_ref[...]
        h_ref[...] = (x * _rms_scale(x) * ln_ref[...]).astype(BF16)

    h = h_ref[...]
    q = jnp.dot(h, wq_ref[...], preferred_element_type=F32)
    gq = gq_ref[...] * (A_SCALE * LOG2E)
    for hh in range(ts // LANES):
        sl = slice(hh * LANES, (hh + 1) * LANES)
        s = q[:, sl]
        q_ref[:, sl] = (s * _rms_scale(s) * gq).astype(BF16)

    k = jnp.dot(h, wk_ref[...], preferred_element_type=F32)
    gk = gk_ref[...]
    k_tails = []
    for hh in range(ts // LANES):
        sl = slice(hh * LANES, (hh + 1) * LANES)
        s = k[:, sl]
        kn = s * _rms_scale(s) * gk
        k_ref[:, sl] = kn.astype(BF16)
        k_tails.append(kn[tm - keep_rows:, :])

    v = jnp.dot(h, wv_ref[...], preferred_element_type=F32)
    v_ref[...] = v.astype(BF16)
    g = jnp.dot(h, wg_ref[...], preferred_element_type=F32)
    sg_ref[...] = (g / (1.0 + jnp.exp(-g))).astype(BF16)

    @pl.when((i % tiles_per_keep) == (tiles_per_keep - 1))
    def _():
        for hh, kn_tail in enumerate(k_tails):
            k32_ref[:, hh * LANES:(hh + 1) * LANES] = kn_tail
        v32_ref[...] = v[tm - keep_rows:, :]


def _proj_a(x, ln, w, gq, gk, *, tm, ts, keep_rows, tiles_per_keep):
    t = x.shape[0]
    nm = t // tm
    nkeep = nm // tiles_per_keep
    nj = A_WIDTH // ts

    def w_spec(part):
        return pl.BlockSpec((D_MODEL, ts), lambda i, j: (0, part * nj + j))

    def tail_map(i, j):
        is_tail = (i % tiles_per_keep) == (tiles_per_keep - 1)
        return (i // tiles_per_keep, jnp.where(is_tail, j, 0))

    kernel = functools.partial(_proj_a_kernel, ts=ts, tiles_per_keep=tiles_per_keep, keep_rows=keep_rows)
    act = jax.ShapeDtypeStruct((t, A_WIDTH), BF16)
    tail = jax.ShapeDtypeStruct((nkeep * keep_rows, A_WIDTH), F32)
    return pl.pallas_call(
        kernel,
        grid=(nm, nj),
        in_specs=[
            pl.BlockSpec((tm, D_MODEL), lambda i, j: (i, 0)),
            _const_spec((1, D_MODEL)),
            w_spec(0), w_spec(1), w_spec(2), w_spec(3),
            _const_spec((1, A_HEAD_DIM)),
            _const_spec((1, A_HEAD_DIM)),
        ],
        out_specs=[pl.BlockSpec((tm, ts), lambda i, j: (i, j))] * 4
                  + [pl.BlockSpec((keep_rows, ts), tail_map)] * 2,
        out_shape=[act] * 4 + [tail] * 2,
        scratch_shapes=[pltpu.VMEM((tm, D_MODEL), BF16)],
        compiler_params=pltpu.CompilerParams(
            dimension_semantics=("arbitrary", "arbitrary"), vmem_limit_bytes=VMEM_LIMIT),
        name="proj_a",
    )(x, ln, w, w, w, w, gq, gk)


def _attn_a_kernel(*refs, has_cache, tq, seq, n_bias, heads):
    if has_cache:
        q_ref, k_ref, v_ref, kc_ref, vc_ref, bias_ref, o_ref, kpad, vpad, s_ref = refs
    else:
        q_ref, k_ref, v_ref, bias_ref, o_ref, kpad, vpad, s_ref = refs
    width = A_WINDOW + tq
    n_blocks = seq // tq

    for hh in range(heads):
        cols = slice(hh * A_HEAD_DIM, (hh + 1) * A_HEAD_DIM)
        if has_cache:
            kpad[:, 0:A_WINDOW] = kc_ref[0, hh].T
            vpad[0:A_WINDOW, :] = vc_ref[0, hh]
        else:
            kpad[:, 0:A_WINDOW] = jnp.zeros((A_HEAD_DIM, A_WINDOW), BF16)
            vpad[0:A_WINDOW, :] = jnp.zeros((A_WINDOW, A_HEAD_DIM), BF16)
        kpad[:, A_WINDOW:A_WINDOW + seq] = k_ref[0, :, cols].T
        vpad[A_WINDOW:A_WINDOW + seq, :] = v_ref[0, :, cols]

        def scores(t):
            q = q_ref[0, t * tq:(t + 1) * tq, cols]
            kb_t = kpad[:, t * tq:t * tq + width]
            s_ref[t % 2] = jnp.dot(q, kb_t, preferred_element_type=F32)

        scores(0)
        for t in range(n_blocks):
            if t + 1 < n_blocks:
                scores(t + 1)
            s = s_ref[t % 2] + bias_ref[hh, min(t, n_bias - 1)]
            m = jnp.max(s, axis=-1, keepdims=True)
            p = jnp.exp2(s - m)
            l = jnp.sum(p, axis=-1, keepdims=True)
            vb = vpad[t * tq:t * tq + width, :]
            o = jnp.dot(p.astype(BF16), vb, preferred_element_type=F32) / l
            o_ref[0, t * tq:(t + 1) * tq, cols] = o.astype(BF16)


def _attn_a(q, k, v, bias, k_cache, v_cache, *, batch, seq, tq, heads):
    has_cache = k_cache is not None
    width = A_WINDOW + tq
    n_bias = bias.shape[1]
    in_specs = [pl.BlockSpec((1, seq, heads * A_HEAD_DIM), lambda b, h: (b, 0, h))] * 3
    args = [a.reshape(batch, seq, A_WIDTH) for a in (q, k, v)]
    if has_cache:
        in_specs += [pl.BlockSpec((1, heads, A_WINDOW, A_HEAD_DIM), lambda b, h: (b, h, 0, 0))] * 2
        args += [k_cache, v_cache]
    in_specs.append(pl.BlockSpec((heads, n_bias, tq, width), lambda b, h: (h, 0, 0, 0)))
    args.append(bias)
    kernel = functools.partial(_attn_a_kernel, has_cache=has_cache, tq=tq, seq=seq, n_bias=n_bias, heads=heads)
    out = pl.pallas_call(
        kernel,
        grid=(batch, A_HEADS // heads),
        in_specs=in_specs,
        out_specs=pl.BlockSpec((1, seq, heads * A_HEAD_DIM), lambda b, h: (b, 0, h)),
        out_shape=jax.ShapeDtypeStruct((batch, seq, A_WIDTH), BF16),
        scratch_shapes=[pltpu.VMEM((A_HEAD_DIM, A_WINDOW + seq), BF16),
                        pltpu.VMEM((A_WINDOW + seq, A_HEAD_DIM), BF16),
                        pltpu.VMEM((2, tq, width), F32)],
        compiler_params=pltpu.CompilerParams(
            dimension_semantics=("arbitrary", "arbitrary"), vmem_limit_bytes=VMEM_LIMIT),
        name="attn_a",
    )(*args)
    return out.reshape(batch * seq, A_WIDTH)


def _band_bias(table, tq, mask_prefix):
    width = A_WINDOW + tq
    length = CHUNK + A_BAND - 1
    n_table = 2 * REL_CLIP + 1
    c0 = A_WINDOW + REL_CLIP - (A_BAND - 1)
    n_lo, first = max(0, -c0), max(0, c0)
    n_hi = length - n_lo - (n_table - first)
    heads = table.shape[0]
    v = jnp.concatenate([jnp.broadcast_to(table[:, :1], (heads, n_lo)), table[:, first:],
                         jnp.broadcast_to(table[:, -1:], (heads, n_hi))], axis=1).astype(F32)
    w = jnp.pad(v[:, ::-1], ((0, 0), (0, 1)))
    skew = jnp.tile(w, (1, CHUNK))[:, :CHUNK * length].reshape(heads, CHUNK, length)
    band = skew[:, :, CHUNK - 1:CHUNK - 1 + A_BAND] * LOG2E
    chunks = [jnp.pad(band, ((0, 0), (0, 0), (c * CHUNK, tq - CHUNK - c * CHUNK)), constant_values=NEG_INF)
              for c in range(tq // CHUNK)]
    block = jnp.concatenate(chunks, axis=1)
    if not mask_prefix:
        return block[:, None]
    j = np.arange(width)[None, :]
    keep = np.stack([np.broadcast_to(j + v_idx * tq >= A_WINDOW, (tq, width))
                     for v_idx in range(A_WINDOW // tq + 1)])
    return jnp.where(keep[None], block[:, None], NEG_INF)


def _out_proj_kernel(x_ref, sg_ref, o_ref, w_ref, y_ref):
    gated = sg_ref[...] * o_ref[...]
    y_ref[...] = x_ref[...] + jnp.dot(gated, w_ref[...], preferred_element_type=F32)


def _out_proj(x, sg, o, w, *, tm):
    t, width = o.shape
    return pl.pallas_call(
        _out_proj_kernel,
        grid=(t // tm,),
        in_specs=[
            pl.BlockSpec((tm, D_MODEL), lambda i: (i, 0)),
            pl.BlockSpec((tm, width), lambda i: (i, 0)),
            pl.BlockSpec((tm, width), lambda i: (i, 0)),
            _const_spec((width, D_MODEL)),
        ],
        out_specs=pl.BlockSpec((tm, D_MODEL), lambda i: (i, 0)),
        out_shape=jax.ShapeDtypeStruct((t, D_MODEL), F32),
        compiler_params=pltpu.CompilerParams(
            dimension_semantics=("arbitrary",), vmem_limit_bytes=VMEM_LIMIT),
        name="out_proj",
    )(x, sg, o, w)


Q_HEADS_PER_DOT = 4
GATE_CHUNK = 512


def _proj_b_kernel(*refs, transposed_q, fused_residual):
    if fused_residual:
        (x_ref, sgp_ref, op_ref, wout_ref, ln_ref, w_ref, wuq_ref, gqa_ref, gkva_ref, gq_ref, gkr_ref, cs_ref,
         csq_ref, y_ref, q_ref, ckv_ref, kr2_ref, sg_ref) = refs
        x = x_ref[...] + jnp.dot(sgp_ref[...] * op_ref[...], wout_ref[...], preferred_element_type=F32)
        y_ref[...] = x
    else:
        (x_ref, ln_ref, w_ref, wuq_ref, gqa_ref, gkva_ref, gq_ref, gkr_ref, cs_ref, csq_ref,
         q_ref, ckv_ref, kr2_ref, sg_ref) = refs
        x = x_ref[...]
    h = (x * _rms_scale(x) * ln_ref[...]).astype(BF16)
    t1 = jnp.dot(h, w_ref[:, :B_LATENT_COLS], preferred_element_type=F32)
    cq = t1[:, :B_Q_RANK]
    ckv = t1[:, B_Q_RANK:B_Q_RANK + B_KV_RANK]
    kr = t1[:, B_Q_RANK + B_KV_RANK:]

    ckv_ref[...] = ckv * _rms_scale(ckv) * gkva_ref[...]
    wk = kr * _rms_scale(kr) * gkr_ref[...] * cs_ref[...]
    kr2_ref[...] = wk + pltpu.roll(wk, B_ROPE, axis=1)

    cqn = (cq * _rms_scale(cq) * gqa_ref[...]).astype(BF16)
    if transposed_q:
        g_nope = gq_ref[:B_NOPE, :] * (B_SCALE * LOG2E)
        g_rope = gq_ref[B_NOPE:, :] * csq_ref[...] * (B_SCALE * LOG2E)
        rows = Q_HEADS_PER_DOT * B_CAT
        for grp in range(B_HEADS // Q_HEADS_PER_DOT):
            qt = lax.dot_general(wuq_ref[grp * rows:(grp + 1) * rows, :], cqn, _NT, preferred_element_type=F32)
            for hh in range(Q_HEADS_PER_DOT):
                head = grp * Q_HEADS_PER_DOT + hh
                qn = qt[hh * B_CAT:hh * B_CAT + B_NOPE, :]
                qr = qt[hh * B_CAT + B_NOPE:(hh + 1) * B_CAT, :]
                sn = lax.rsqrt(jnp.mean(qn * qn, axis=0, keepdims=True) + EPS)
                sr = lax.rsqrt(jnp.mean(qr * qr, axis=0, keepdims=True) + EPS)
                q_ref[0, head, 0:B_NOPE, :] = (qn * sn * g_nope).astype(BF16)
                q_ref[0, head, B_NOPE:B_CAT, :] = (qr * sr * g_rope).astype(BF16)
    else:
        q = jnp.dot(cqn, wuq_ref[...], preferred_element_type=F32)
        g_nope = gq_ref[:, :B_NOPE] * (B_SCALE * LOG2E)
        g_rope = gq_ref[:, B_NOPE:] * csq_ref[...] * (B_SCALE * LOG2E)
        for hh in range(B_HEADS):
            qn = q[:, hh * B_CAT:hh * B_CAT + B_NOPE]
            qr = q[:, hh * B_CAT + B_NOPE:(hh + 1) * B_CAT]
            q_ref[:, hh * B_CAT:hh * B_CAT + B_NOPE] = (qn * _rms_scale(qn) * g_nope).astype(BF16)
            q_ref[:, hh * B_CAT + B_NOPE:(hh + 1) * B_CAT] = (qr * _rms_scale(qr) * g_rope).astype(BF16)

    for c in range(B_WIDTH // GATE_CHUNK):
        sl = slice(c * GATE_CHUNK, (c + 1) * GATE_CHUNK)
        g = jnp.dot(h, w_ref[:, B_LATENT_COLS + c * GATE_CHUNK:B_LATENT_COLS + (c + 1) * GATE_CHUNK],
                    preferred_element_type=F32)
        sg_ref[:, sl] = (g / (1.0 + jnp.exp(-g))).astype(BF16)


def _proj_b(x, ln, w, wuq, gqa, gkva, gq, gkr, cs, *, tm, batch=None, pending=None):
    t = x.shape[0]
    n_pos_tiles = cs.shape[0] // tm
    transposed_q = batch is not None
    pos_spec = pl.BlockSpec((tm, 2 * B_ROPE), lambda i: (i % n_pos_tiles, 0))
    if transposed_q:
        tiles = t // batch // tm
        wuq = wuq.T
        gq = jnp.broadcast_to(gq.reshape(B_CAT, 1), (B_CAT, tm))
        csq = cs.T
        gq_spec = _const_spec((B_CAT, tm))
        csq_spec = pl.BlockSpec((2 * B_ROPE, tm), lambda i: (0, i % n_pos_tiles))
        q_spec = pl.BlockSpec((1, B_HEADS, B_CAT, tm), lambda i: (i // tiles, 0, 0, i % tiles))
        q_shape = jax.ShapeDtypeStruct((batch, B_HEADS, B_CAT, t // batch), BF16)
    else:
        csq = cs
        gq_spec = _const_spec((1, B_CAT))
        csq_spec = pos_spec
        q_spec = pl.BlockSpec((tm, B_HEADS * B_CAT), lambda i: (i, 0))
        q_shape = jax.ShapeDtypeStruct((t, B_HEADS * B_CAT), BF16)
    row_spec = lambda width: pl.BlockSpec((tm, width), lambda i: (i, 0))
    in_specs = [row_spec(D_MODEL)]
    args = [x]
    out_specs, out_shape = [], []
    if pending is not None:
        sg_prev, o_prev, w_out = pending
        in_specs += [row_spec(sg_prev.shape[1]), row_spec(o_prev.shape[1]), _const_spec(w_out.shape)]
        args += [sg_prev, o_prev, w_out]
        out_specs.append(row_spec(D_MODEL))
        out_shape.append(jax.ShapeDtypeStruct((t, D_MODEL), F32))
    in_specs += [
        _const_spec((1, D_MODEL)),
        _const_spec(w.shape),
        _const_spec(wuq.shape),
        _const_spec((1, B_Q_RANK)),
        _const_spec((1, B_KV_RANK)),
        gq_spec,
        _const_spec((1, 2 * B_ROPE)),
        pos_spec,
        csq_spec,
    ]
    args += [ln, w, wuq, gqa, gkva, gq, gkr, cs, csq]
    out_specs += [q_spec, row_spec(B_KV_RANK), row_spec(2 * B_ROPE), row_spec(B_WIDTH)]
    out_shape += [
        q_shape,
        jax.ShapeDtypeStruct((t, B_KV_RANK), F32),
        jax.ShapeDtypeStruct((t, 2 * B_ROPE), F32),
        jax.ShapeDtypeStruct((t, B_WIDTH), BF16),
    ]
    return pl.pallas_call(
        functools.partial(_proj_b_kernel, transposed_q=transposed_q, fused_residual=pending is not None),
        grid=(t // tm,),
        in_specs=in_specs,
        out_specs=out_specs,
        out_shape=out_shape,
        compiler_params=pltpu.CompilerParams(
            dimension_semantics=("arbitrary",), vmem_limit_bytes=VMEM_LIMIT),
        name="proj_b",
    )(*args)


def _expand_b_kernel(ckv_ref, kr2_ref, wuk_ref, wuv_ref, gkn_ref, kcat_ref, v_ref, *, transposed_v):
    c = ckv_ref[...].astype(BF16)
    kn = jnp.dot(c, wuk_ref[...], preferred_element_type=F32)
    krb = kr2_ref[...].astype(BF16)
    g = gkn_ref[...]
    for hh in range(B_HEADS):
        s = kn[:, hh * B_NOPE:(hh + 1) * B_NOPE]
        kcat_ref[:, hh * B_CAT:hh * B_CAT + B_NOPE] = (s * _rms_scale(s) * g).astype(BF16)
        kcat_ref[:, hh * B_CAT + B_NOPE:(hh + 1) * B_CAT] = krb
    if transposed_v:
        vt = lax.dot_general(wuv_ref[...], c, _NT, preferred_element_type=F32).astype(BF16)
        tm = c.shape[0]
        for hh in range(B_HEADS):
            v_ref[0, hh, 0, 0:B_V, :] = vt[hh * B_V:(hh + 1) * B_V, :]
            v_ref[0, hh, 0, B_V:VT_ROWS, :] = jnp.ones((VT_ROWS - B_V, tm), BF16)
    else:
        v_ref[...] = jnp.dot(c, wuv_ref[...], preferred_element_type=F32).astype(BF16)


def _expand_b(ckv, kr2, wuk, wuv, gkn, *, tm, batch=None):
    t = ckv.shape[0]
    transposed_v = batch is not None
    if transposed_v:
        tiles = t // batch // tm
        v_spec = pl.BlockSpec((1, B_HEADS, 1, VT_ROWS, tm), lambda i: (i // tiles, 0, i % tiles, 0, 0))
        v_shape = jax.ShapeDtypeStruct((batch, B_HEADS, tiles, VT_ROWS, tm), BF16)
    else:
        v_spec = pl.BlockSpec((tm, B_WIDTH), lambda i: (i, 0))
        v_shape = jax.ShapeDtypeStruct((t, B_WIDTH), BF16)
    return pl.pallas_call(
        functools.partial(_expand_b_kernel, transposed_v=transposed_v),
        grid=(t // tm,),
        in_specs=[
            pl.BlockSpec((tm, B_KV_RANK), lambda i: (i, 0)),
            pl.BlockSpec((tm, 2 * B_ROPE), lambda i: (i, 0)),
            _const_spec(wuk.shape),
            _const_spec(wuv.shape),
            _const_spec((1, B_NOPE)),
        ],
        out_specs=[pl.BlockSpec((tm, B_HEADS * B_CAT), lambda i: (i, 0)), v_spec],
        out_shape=[jax.ShapeDtypeStruct((t, B_HEADS * B_CAT), BF16), v_shape],
        compiler_params=pltpu.CompilerParams(
            dimension_semantics=("arbitrary",), vmem_limit_bytes=VMEM_LIMIT),
        name="expand_b",
    )(ckv, kr2, wuk, wuv, gkn)


def _attn_bt_kernel(q_ref, k_ref, vt_ref, o_ref, s_ref, p_ref, alpha_ref, mask_ref, m_ref, acc_ref,
                    *, tq, n_blocks, kv_blocks, heads):
    pairs = [(qi, k0, min(kv_blocks, qi + 1 - k0))
             for qi in range(n_blocks) for k0 in range(0, qi + 1, kv_blocks)]

    key_chunk = lax.broadcasted_iota(jnp.int32, (tq, tq), 0) // CHUNK
    q_chunk = lax.broadcasted_iota(jnp.int32, (tq, tq), 1) // CHUNK
    mask_ref[...] = jnp.where(key_chunk <= q_chunk, 0.0, NEG_INF).astype(F32)

    half = tq // 2

    def stages(hh):
        s_h, p_h, alpha_h = s_ref.at[hh], p_ref.at[hh], alpha_ref.at[hh]
        m_h, acc_h = m_ref.at[hh], acc_ref.at[hh]

        def scores(t):
            qi, k0, nk = pairs[t]
            slot = t % 2
            q_t = q_ref[0, hh, :, qi * tq:(qi + 1) * tq]
            rows = nk * tq
            kb = k_ref[0, k0 * tq:k0 * tq + rows, hh * B_CAT:(hh + 1) * B_CAT]
            if k0 + nk - 1 == qi:
                full = rows - half
                s_h[slot, :full, :] = jnp.dot(kb[:full], q_t, preferred_element_type=F32)
                s_h[slot, full:rows, half:] = jnp.dot(kb[full:], q_t[:, half:], preferred_element_type=F32)
            else:
                s_h[slot, :rows, :] = jnp.dot(kb, q_t, preferred_element_type=F32)

        def regions(t):
            qi, k0, nk = pairs[t]
            rows = nk * tq
            if k0 + nk - 1 != qi:
                return [(0, rows, 0, tq, None)]
            base = rows - tq
            out = [(0, base, 0, tq, None)] if base else []
            return out + [(base, base + half, 0, half, 0), (base, base + half, half, tq, None),
                          (base + half, rows, half, tq, half)]

        def probabilities(t):
            qi, k0, nk = pairs[t]
            slot = t % 2
            parts = []
            for r0, r1, c0, c1, mrow in regions(t):
                s = s_h[slot, r0:r1, c0:c1]
                if mrow is not None:
                    s = s + mask_ref[mrow:mrow + (r1 - r0), c0:c1]
                parts.append(s)
            col_max = jnp.full((1, tq), NEG_INF, F32)
            for (r0, r1, c0, c1, _), s in zip(regions(t), parts):
                part_max = jnp.max(s, axis=0, keepdims=True)
                if c1 - c0 < tq:
                    pad = jnp.full((1, tq - (c1 - c0)), NEG_INF, F32)
                    part_max = jnp.concatenate([part_max, pad] if c0 == 0 else [pad, part_max], axis=1)
                col_max = jnp.maximum(col_max, part_max)
            if k0 == 0:
                m_new = col_max
            else:
                m_new = jnp.maximum(m_h[...], col_max)
                alpha_h[slot] = jnp.exp2(m_h[...] - m_new)
            m_h[...] = m_new
            for (r0, r1, c0, c1, _), s in zip(regions(t), parts):
                p_h[slot, r0:r1, c0:c1] = jnp.exp2(s - m_new[:, c0:c1]).astype(BF16)

        def values(t):
            qi, k0, nk = pairs[t]
            slot = t % 2
            diag = k0 + nk - 1 == qi

            def product(c0, c1, last_rows):
                total = None
                for b in range(nk):
                    n = last_rows if b == nk - 1 else tq
                    term = jnp.dot(vt_ref[0, hh, k0 + b, :, :n], p_h[slot, b * tq:b * tq + n, c0:c1],
                                   preferred_element_type=F32)
                    total = term if total is None else total + term
                return total

            if diag:
                pv = jnp.concatenate([product(0, half, half), product(half, tq, tq)], axis=1)
            else:
                pv = product(0, tq, tq)
            acc = pv if k0 == 0 else alpha_h[slot] * acc_h[...] + pv
            if diag:
                o_t = acc[:B_V, :] / acc[B_V:B_V + 1, :]
                o_ref[0, qi * tq:(qi + 1) * tq, hh * B_V:(hh + 1) * B_V] = o_t.T.astype(BF16)
            else:
                acc_h[...] = acc
        return scores, probabilities, values

    head_stages = [stages(hh) for hh in range(heads)]
    for scores, _, _ in head_stages:
        scores(0)
    for t in range(len(pairs) + 1):
        for scores, probabilities, values in head_stages:
            if t + 1 < len(pairs):
                scores(t + 1)
            if t < len(pairs):
                probabilities(t)
            if t >= 1:
                values(t - 1)


def _attn_bt(qt, kcat, vt, *, batch, seq, tq, kv_blocks, heads):
    k3 = kcat.reshape(batch, seq, B_HEADS * B_CAT)
    n_blocks = seq // tq
    out = pl.pallas_call(
        functools.partial(_attn_bt_kernel, tq=tq, n_blocks=n_blocks, kv_blocks=kv_blocks, heads=heads),
        grid=(batch, B_HEADS // heads),
        in_specs=[
            pl.BlockSpec((1, heads, B_CAT, seq), lambda b, h: (b, h, 0, 0)),
            pl.BlockSpec((1, seq, heads * B_CAT), lambda b, h: (b, 0, h)),
            pl.BlockSpec((1, heads, n_blocks, VT_ROWS, tq), lambda b, h: (b, h, 0, 0, 0)),
        ],
        out_specs=pl.BlockSpec((1, seq, heads * B_V), lambda b, h: (b, 0, h)),
        out_shape=jax.ShapeDtypeStruct((batch, seq, B_WIDTH), BF16),
        scratch_shapes=[pltpu.VMEM((heads, 2, kv_blocks * tq, tq), F32),
                        pltpu.VMEM((heads, 2, kv_blocks * tq, tq), BF16),
                        pltpu.VMEM((heads, 2, 1, tq), F32), pltpu.VMEM((tq, tq), F32),
                        pltpu.VMEM((heads, 1, tq), F32), pltpu.VMEM((heads, VT_ROWS, tq), F32)],
        compiler_params=pltpu.CompilerParams(
            dimension_semantics=("arbitrary", "arbitrary"), vmem_limit_bytes=VMEM_LIMIT),
        name="attn_bt",
    )(qt, k3, vt)
    return out.reshape(batch * seq, B_WIDTH)


def _attn_b_kernel(q_ref, k_ref, v_ref, o_ref):
    for hh in range(B_HEADS):
        q = q_ref[0, :, hh * B_CAT:(hh + 1) * B_CAT]
        kb = k_ref[0, :, hh * B_CAT:(hh + 1) * B_CAT]
        s = lax.dot_general(q, kb, _NT, preferred_element_type=F32)
        p = jnp.exp2(s - jnp.max(s, axis=-1, keepdims=True))
        l = jnp.sum(p, axis=-1, keepdims=True)
        vb = v_ref[0, :, hh * B_V:(hh + 1) * B_V]
        o = jnp.dot(p.astype(BF16), vb, preferred_element_type=F32) / l
        o_ref[0, :, hh * B_V:(hh + 1) * B_V] = o.astype(BF16)


def _attn_b(qcat, kcat, v, *, batch, q_len, k_len):
    q3 = qcat.reshape(batch, q_len, B_HEADS * B_CAT)
    k3 = kcat.reshape(batch, k_len, B_HEADS * B_CAT)
    v3 = v.reshape(batch, k_len, B_WIDTH)
    out = pl.pallas_call(
        _attn_b_kernel,
        grid=(batch,),
        in_specs=[
            pl.BlockSpec((1, q_len, B_HEADS * B_CAT), lambda b: (b, 0, 0)),
            pl.BlockSpec((1, k_len, B_HEADS * B_CAT), lambda b: (b, 0, 0)),
            pl.BlockSpec((1, k_len, B_WIDTH), lambda b: (b, 0, 0)),
        ],
        out_specs=pl.BlockSpec((1, q_len, B_WIDTH), lambda b: (b, 0, 0)),
        out_shape=jax.ShapeDtypeStruct((batch, q_len, B_WIDTH), BF16),
        compiler_params=pltpu.CompilerParams(
            dimension_semantics=("arbitrary",), vmem_limit_bytes=VMEM_LIMIT),
        name="attn_b",
    )(q3, k3, v3)
    return out.reshape(batch * q_len, B_WIDTH)


def _swap_halves(a):
    half = a.shape[-1] // 2
    return jnp.concatenate([a[..., half:], a[..., :half]], axis=-1)


def _rope_table(pos):
    half = B_ROPE // 2
    inv = ROPE_THETA ** (-jnp.arange(half, dtype=F32) / half)
    ang = pos.astype(F32)[:, None] * inv[None, :]
    cos, sin = jnp.cos(ang), jnp.sin(ang)
    return jnp.concatenate([cos, cos, -sin, sin], axis=-1)


def _row(v):
    return v.reshape(1, -1).astype(F32)


def _layer_a(x, k_cache, v_cache, ln, w_in, gq, gk, table, w_out, *, batch, seq, tm, tq, tm_out, defer_out):
    if k_cache is None:
        keep = min(A_WINDOW, seq)
        keep_rows, tiles_per_keep = keep, seq // tm
    else:
        keep = seq
        keep_rows, tiles_per_keep = tm, 1
    q, k, v, sg, k32, v32 = _proj_a(x, ln, w_in, gq, gk, tm=tm, ts=A_PROJ_SEG,
                                    keep_rows=keep_rows, tiles_per_keep=tiles_per_keep)
    bias = _band_bias(table, tq, mask_prefix=k_cache is None)
    o = _attn_a(q, k, v, bias, k_cache, v_cache, batch=batch, seq=seq, tq=tq,
                heads=1 if k_cache is None else A_HEADS)
    shape = (batch, keep, A_HEADS, A_HEAD_DIM)
    if defer_out:
        return x, (sg, o, w_out), k32.reshape(shape), v32.reshape(shape)
    y = _out_proj(x, sg, o, w_out, tm=tm_out)
    return y, None, k32.reshape(shape), v32.reshape(shape)


def _layer_b(x, pending, ckv_cache, kr_cache, pos, ln, w, wuq, gqa, gkva, gq, gkr, wuk, wuv, gkn, w_out,
             *, batch, seq, tm, tm_out):
    cs = _rope_table(pos)
    if ckv_cache is not None:
        cs = jnp.tile(cs, (batch, 1))
    outs = _proj_b(x, ln, w, wuq, gqa, gkva, gq, gkr, cs, tm=tm,
                   batch=batch if ckv_cache is None else None, pending=pending)
    if pending is not None:
        x, *outs = outs
    qcat, ckv, kr2, sg = outs
    if ckv_cache is None:
        kcat, vt = _expand_b(ckv, kr2, wuk, wuv.T, gkn, tm=B_ATTN_ROWS, batch=batch)
        o = _attn_bt(qcat, kcat, vt, batch=batch, seq=seq, tq=B_ATTN_ROWS, kv_blocks=B_KEY_BLOCKS, heads=1)
    else:
        k_len = ckv_cache.shape[1] + seq
        ckv_all = jnp.concatenate(
            [ckv_cache, ckv.reshape(batch, seq, B_KV_RANK)], axis=1).reshape(batch * k_len, B_KV_RANK)
        kr2_all = jnp.concatenate(
            [jnp.concatenate([kr_cache, kr_cache], axis=-1), kr2.reshape(batch, seq, 2 * B_ROPE)],
            axis=1).reshape(batch * k_len, 2 * B_ROPE)
        kcat, v = _expand_b(ckv_all, kr2_all, wuk, wuv, gkn, tm=B_EXPAND_ROWS)
        o = _attn_b(qcat, kcat, v, batch=batch, q_len=seq, k_len=k_len)
    y = _out_proj(x, sg, o, w_out, tm=tm_out)
    return y, ckv.reshape(batch, seq, B_KV_RANK), kr2[:, :B_ROPE].reshape(batch, seq, B_ROPE)


def kernel(x_prompt, x_sample, cache_a_k, cache_a_v, cache_b_ckv, cache_b_krope, a_ln, w_a_in, a_q_norm, a_k_norm, a_rel_bias, w_a_out, b_ln, w_b_in, b_q_a_norm, w_b_uq, b_kv_a_norm, w_b_uk, w_b_uv, b_q_nope_norm, b_k_nope_norm, b_q_rope_norm, b_k_rope_norm, w_b_out):
    batch, seq, _ = x_prompt.shape
    dec_batch, dec_seq, _ = x_sample.shape
    dec_tokens = dec_batch * dec_seq
    depth = a_ln.shape[0] + b_ln.shape[0]
    yp = x_prompt.reshape(batch * seq, D_MODEL)
    ys = x_sample.reshape(dec_tokens, D_MODEL)
    akp, avp, aks, avs, bcp, brp, bcs, brs = ([] for _ in range(8))
    for layer in range(depth):
        i = layer // 2
        if layer % 2 == 0:
            params = (_row(a_ln[i]), w_a_in[i].astype(BF16), _row(a_q_norm[i]), _row(a_k_norm[i]),
                      a_rel_bias[i], w_a_out[i].astype(BF16))
            yp, pending_p, k_p, v_p = _layer_a(yp, None, None, *params, batch=batch, seq=seq,
                                               tm=A_PROJ_ROWS, tq=A_Q_ROWS, tm_out=OUT_ROWS,
                                               defer_out=layer + 1 < depth)
            kc = cache_a_k[i].transpose(0, 2, 1, 3).astype(BF16)
            vc = cache_a_v[i].transpose(0, 2, 1, 3).astype(BF16)
            ys, pending_s, k_s, v_s = _layer_a(ys, kc, vc, *params, batch=dec_batch, seq=dec_seq,
                                               tm=dec_tokens, tq=CHUNK, tm_out=dec_tokens,
                                               defer_out=layer + 1 < depth)
            akp.append(k_p); avp.append(v_p); aks.append(k_s); avs.append(v_s)
        else:
            w_in = w_b_in[i]
            c1, c2 = B_Q_RANK + B_KV_RANK, B_Q_RANK + B_KV_RANK + B_ROPE
            w = jnp.concatenate([w_in[:, :c2], _swap_halves(w_in[:, c1:c2]), w_in[:, c2:]], axis=1).astype(BF16)
            wuq = w_b_uq[i].reshape(B_Q_RANK, B_HEADS, B_QK)
            wuq_rope = wuq[..., B_NOPE:]
            wuq = jnp.concatenate([wuq[..., :B_NOPE], wuq_rope, _swap_halves(wuq_rope)],
                                  axis=-1).reshape(B_Q_RANK, B_HEADS * B_CAT).astype(BF16)
            g_qr, g_kr = b_q_rope_norm[i], b_k_rope_norm[i]
            gq = _row(jnp.concatenate([b_q_nope_norm[i], g_qr, _swap_halves(g_qr)]))
            gkr = _row(jnp.concatenate([g_kr, _swap_halves(g_kr)]))
            params = (_row(b_ln[i]), w, wuq, _row(b_q_a_norm[i]), _row(b_kv_a_norm[i]), gq, gkr,
                      w_b_uk[i].astype(BF16), w_b_uv[i].astype(BF16), _row(b_k_nope_norm[i]),
                      w_b_out[i].astype(BF16))
            yp, c_p, r_p = _layer_b(yp, pending_p, None, None, jnp.arange(seq), *params, batch=batch, seq=seq,
                                    tm=B_PROJ_ROWS if pending_p is None else B_FUSED_PROJ_ROWS, tm_out=OUT_ROWS)
            pending_p = None
            ys, c_s, r_s = _layer_b(ys, pending_s, cache_b_ckv[i], cache_b_krope[i], PAST_LEN + jnp.arange(dec_seq),
                                    *params, batch=dec_batch, seq=dec_seq, tm=B_SAMPLE_PROJ_ROWS,
                                    tm_out=dec_tokens)
            pending_s = None
            bcp.append(c_p); brp.append(r_p); bcs.append(c_s); brs.append(r_s)
    yp = yp.reshape(batch, seq, D_MODEL)
    ys = ys.reshape(dec_batch, dec_seq, D_MODEL)
    return (yp, ys,
            jnp.stack(akp), jnp.stack(avp), jnp.stack(bcp), jnp.stack(brp),
            jnp.stack(aks), jnp.stack(avs), jnp.stack(bcs), jnp.stack(brs))
```
